```python
import jax, jax.numpy as jnp
from jax import lax
import numpy as np

D_MODEL = 1024
BATCH = 8
SEQ = 2048
DEPTH = 2
DEC_BATCH = 128
DEC_SEQ = 1
PAST_LEN = 16384
PAGE_SIZE = 128

N_MIXERS = 2
N_CONV_LAYERS = (DEPTH + 1) // 2
N_RWKV_LAYERS = DEPTH // 2
N_META = 16
CONV_WIDTH = 31
CONV_BUF = CONV_WIDTH - 1
HEAD_SIZE = 64
N_HEADS = D_MODEL // HEAD_SIZE
D_FF = 4 * D_MODEL
DECAY_LORA = 64
AAA_LORA = 64
GATE_LORA = 128
RMS_EPS = 1e-6
LN_EPS = 1e-5
GN_EPS = 64e-5
L2_EPS = 1e-12

kernel_name = "hybrid_conformer_conv_rwkv7_step"


def _rmsnorm(x, g):
    xf = x.astype(jnp.float32)
    y = xf * lax.rsqrt(jnp.mean(xf * xf, axis=-1, keepdims=True) + RMS_EPS)
    return (y * g.astype(jnp.float32)).astype(x.dtype)


def _conformer_conv(h, buf, w_pw1, b_pw1, w_dw, b_dw, ln_g, ln_b, w_pw2, b_pw2):
    u = h @ w_pw1 + b_pw1
    u = u[..., :D_MODEL] * jax.nn.sigmoid(u[..., D_MODEL:])
    full = jnp.concatenate([buf.astype(u.dtype), u], axis=1)
    c = lax.conv_general_dilated(
        full, w_dw[:, None, :].astype(full.dtype), window_strides=(1,), padding='VALID',
        dimension_numbers=('NWC', 'WIO', 'NWC'), feature_group_count=D_MODEL) + b_dw
    cf = c.astype(jnp.float32)
    mu = jnp.mean(cf, axis=-1, keepdims=True)
    var = jnp.mean(jnp.square(cf - mu), axis=-1, keepdims=True)
    cn = ((cf - mu) * lax.rsqrt(var + LN_EPS) * ln_g.astype(jnp.float32) + ln_b.astype(jnp.float32)).astype(h.dtype)
    out = jax.nn.silu(cn) @ w_pw2 + b_pw2
    return out, full[:, -CONV_BUF:]


def _wkv7_step(S, inp):
    r_t, w_t, k_t, v_t, kk_t, a_t = inp
    sa = jnp.einsum('bhvk,bhk->bhv', S, -kk_t)
    S = (S * w_t[:, :, None, :] + sa[..., None] * (kk_t * a_t)[:, :, None, :]
         + v_t[..., None] * k_t[:, :, None, :])
    y = jnp.einsum('bhvk,bhk->bhv', S, r_t)
    return S, y


def _rwkv7(h, shift_prev, S0, x_mix, w_r, w_k, w_v, w_o, w0, w1, w2, a0, a1, a2,
           g1, g2, k_k, k_a, r_k, gn_g, gn_b):
    B, T, _ = h.shape
    f32 = jnp.float32
    prev = jnp.concatenate([shift_prev[:, None, :].astype(h.dtype), h[:, :-1]], axis=1)
    xx = prev - h
    xr = h + xx * x_mix[0]
    xw = h + xx * x_mix[1]
    xk = h + xx * x_mix[2]
    xv = h + xx * x_mix[3]
    xa = h + xx * x_mix[4]
    xg = h + xx * x_mix[5]
    r = (xr @ w_r).astype(f32)
    k = (xk @ w_k).astype(f32)
    v = (xv @ w_v).astype(f32)
    w_log = -jax.nn.softplus(-(w0 + jnp.tanh(xw @ w1) @ w2).astype(f32)) - 0.5
    decay = jnp.exp(-jnp.exp(w_log))
    a = jax.nn.sigmoid((a0 + (xa @ a1) @ a2).astype(f32))
    g = jax.nn.sigmoid(xg @ g1) @ g2
    hs = (B, T, N_HEADS, HEAD_SIZE)
    kk = (k * k_k.astype(f32)).reshape(hs)
    kk = kk / jnp.maximum(jnp.sqrt(jnp.sum(kk * kk, axis=-1, keepdims=True)), L2_EPS)
    k = k * (1.0 + (a - 1.0) * k_a.astype(f32))
    r, k, v, decay, a = (t.reshape(hs) for t in (r, k, v, decay, a))
    xs = tuple(jnp.moveaxis(t, 1, 0) for t in (r, decay, k, v, kk, a))
    S_final, ys = lax.scan(_wkv7_step, S0.astype(f32), xs)
    y = jnp.moveaxis(ys, 0, 1)
    mu = jnp.mean(y, axis=-1, keepdims=True)
    var = jnp.mean(jnp.square(y - mu), axis=-1, keepdims=True)
    yn = ((y - mu) * lax.rsqrt(var + GN_EPS)).reshape(B, T, D_MODEL)
    yn = yn * gn_g.astype(f32) + gn_b.astype(f32)
    bonus = (jnp.sum(r * k * r_k.astype(f32), axis=-1, keepdims=True) * v).reshape(B, T, D_MODEL)
    out = ((yn + bonus).astype(h.dtype) * g) @ w_o
    return out, h[:, -1], S_final.astype(S0.dtype)


def _sqrelu_mlp(h, w_in, w_out):
    return jnp.square(jax.nn.relu(h @ w_in)) @ w_out


def _trunk(h, conv_bufs, shift_bufs, wkv_states, norm_mix, norm_mlp, norm_final,
           conv_params, rwkv_params, w_mlp_in, w_mlp_out):
    new_conv, new_shift, new_wkv = [], [], []
    for i in range(DEPTH):
        j = i // N_MIXERS
        hn = _rmsnorm(h, norm_mix[i])
        if i % N_MIXERS == 0:
            out, buf = _conformer_conv(hn, conv_bufs[j], *[p[j] for p in conv_params])
            new_conv.append(buf)
        else:
            out, sh, S = _rwkv7(hn, shift_bufs[j], wkv_states[j], *[p[j] for p in rwkv_params])
            new_shift.append(sh)
            new_wkv.append(S)
        h = h + out
        h = h + _sqrelu_mlp(_rmsnorm(h, norm_mlp[i]), w_mlp_in[i], w_mlp_out[i])
    return (_rmsnorm(h, norm_final), jnp.stack(new_conv, 0), jnp.stack(new_shift, 0),
            jnp.stack(new_wkv, 0))


def setup_inputs(seed: int = 0) -> dict:
    key = jax.random.key(seed)
    ks = iter(jax.random.split(key, 64))
    D = D_MODEL
    NC = N_CONV_LAYERS
    NR = N_RWKV_LAYERS

    def nrm(shape, scale):
        return jax.random.normal(next(ks), shape, jnp.float32) * scale

    def uni(shape, lo, hi):
        return jax.random.uniform(next(ks), shape, jnp.float32, lo, hi)

    return {
        "x_prompt": nrm((BATCH, SEQ, D), 1.0),
        "x_sample": nrm((DEC_BATCH, DEC_SEQ, D), 1.0),
        "state_conv": nrm((NC, DEC_BATCH, CONV_BUF, D), 0.5),
        "state_shift": nrm((NR, DEC_BATCH, D), 1.0),
        "state_wkv": nrm((NR, DEC_BATCH, N_HEADS, HEAD_SIZE, HEAD_SIZE), 0.3),
        "meta_tokens": nrm((N_META, D), 1.0),
        "norm_mix": 1.0 + nrm((DEPTH, D), 0.05),
        "norm_mlp": 1.0 + nrm((DEPTH, D), 0.05),
        "norm_final": 1.0 + nrm((D,), 0.05),
        "conv_w_pw1": nrm((NC, D, 2 * D), D ** -0.5),
        "conv_b_pw1": nrm((NC, 2 * D), 0.02),
        "conv_w_dw": nrm((NC, CONV_WIDTH, D), CONV_WIDTH ** -0.5),
        "conv_b_dw": nrm((NC, D), 0.02),
        "conv_ln_g": 1.0 + nrm((NC, D), 0.05),
        "conv_ln_b": nrm((NC, D), 0.02),
        "conv_w_pw2": nrm((NC, D, D), D ** -0.5),
        "conv_b_pw2": nrm((NC, D), 0.02),
        "rwkv_x_mix": uni((NR, 6, D), 0.0, 1.0),
        "rwkv_w_r": nrm((NR, D, D), D ** -0.5),
        "rwkv_w_k": nrm((NR, D, D), D ** -0.5),
        "rwkv_w_v": nrm((NR, D, D), D ** -0.5),
        "rwkv_w_o": nrm((NR, D, D), D ** -0.5),
        "rwkv_w0": uni((NR, D), -3.0, 1.0),
        "rwkv_w1": nrm((NR, D, DECAY_LORA), 0.5 * D ** -0.5),
        "rwkv_w2": nrm((NR, DECAY_LORA, D), 0.5 * DECAY_LORA ** -0.5),
        "rwkv_a0": nrm((NR, D), 0.1),
        "rwkv_a1": nrm((NR, D, AAA_LORA), 0.5 * D ** -0.5),
        "rwkv_a2": nrm((NR, AAA_LORA, D), 0.5 * AAA_LORA ** -0.5),
        "rwkv_g1": nrm((NR, D, GATE_LORA), D ** -0.5),
        "rwkv_g2": nrm((NR, GATE_LORA, D), GATE_LORA ** -0.5),
        "rwkv_k_k": 0.85 + nrm((NR, D), 0.05),
        "rwkv_k_a": 1.0 + nrm((NR, D), 0.05),
        "rwkv_r_k": nrm((NR, N_HEADS, HEAD_SIZE), 0.1),
        "rwkv_gn_g": 1.0 + nrm((NR, D), 0.05),
        "rwkv_gn_b": nrm((NR, D), 0.02),
        "w_mlp_in": nrm((DEPTH, D, D_FF), D ** -0.5),
        "w_mlp_out": nrm((DEPTH, D_FF, D), D_FF ** -0.5),
    }


def reference(x_prompt, x_sample, state_conv, state_shift, state_wkv, meta_tokens,
              norm_mix, norm_mlp, norm_final,
              conv_w_pw1, conv_b_pw1, conv_w_dw, conv_b_dw, conv_ln_g, conv_ln_b,
              conv_w_pw2, conv_b_pw2,
              rwkv_x_mix, rwkv_w_r, rwkv_w_k, rwkv_w_v, rwkv_w_o, rwkv_w0, rwkv_w1, rwkv_w2,
              rwkv_a0, rwkv_a1, rwkv_a2, rwkv_g1, rwkv_g2, rwkv_k_k, rwkv_k_a, rwkv_r_k,
              rwkv_gn_g, rwkv_gn_b,
              w_mlp_in, w_mlp_out):
    conv_params = (conv_w_pw1, conv_b_pw1, conv_w_dw, conv_b_dw, conv_ln_g, conv_ln_b,
                   conv_w_pw2, conv_b_pw2)
    rwkv_params = (rwkv_x_mix, rwkv_w_r, rwkv_w_k, rwkv_w_v, rwkv_w_o, rwkv_w0, rwkv_w1, rwkv_w2,
                   rwkv_a0, rwkv_a1, rwkv_a2, rwkv_g1, rwkv_g2, rwkv_k_k, rwkv_k_a, rwkv_r_k,
                   rwkv_gn_g, rwkv_gn_b)
    B = x_prompt.shape[0]
    dt = x_prompt.dtype
    meta = jnp.broadcast_to(meta_tokens[None].astype(dt), (B, N_META, D_MODEL))
    hp = jnp.concatenate([meta, x_prompt], axis=1)
    zero_conv = jnp.zeros((N_CONV_LAYERS, B, CONV_BUF, D_MODEL), dt)
    zero_shift = jnp.zeros((N_RWKV_LAYERS, B, D_MODEL), dt)
    zero_wkv = jnp.zeros((N_RWKV_LAYERS, B, N_HEADS, HEAD_SIZE, HEAD_SIZE), state_wkv.dtype)
    yp, conv_prompt, shift_prompt, wkv_prompt = _trunk(
        hp, zero_conv, zero_shift, zero_wkv, norm_mix, norm_mlp, norm_final,
        conv_params, rwkv_params, w_mlp_in, w_mlp_out)
    y_prompt = yp[:, N_META:]
    y_sample, conv_sample, shift_sample, wkv_sample = _trunk(
        x_sample, state_conv, state_shift, state_wkv, norm_mix, norm_mlp, norm_final,
        conv_params, rwkv_params, w_mlp_in, w_mlp_out)
    return (y_prompt, y_sample, conv_prompt, shift_prompt, wkv_prompt,
            conv_sample, shift_sample, wkv_sample)
```

```python
import functools
import math

import jax
import jax.numpy as jnp
from jax import lax
from jax.experimental import pallas as pl
from jax.experimental.pallas import tpu as pltpu

F32 = jnp.float32
BF16 = jnp.bfloat16

RMS_EPS = 1e-6
LN_EPS = 1e-5
GN_EPS = 64e-5
L2_EPS = 1e-12
HEAD = 64
CONV_W = 31
CONV_BUF = CONV_W - 1
BUF_PAD = 32
MXU_TILE = 256
LANE = 128
VMEM_LIMIT = 52 * 1024 * 1024


def _mm(a, w):
    return jnp.dot(a.astype(BF16), w, preferred_element_type=F32)


def _dot_nt(a, b):
    return lax.dot_general(a, b, (((1,), (1,)), ((), ())), preferred_element_type=F32)


def _dot_tn(a, b):
    return lax.dot_general(a, b, (((0,), (0,)), ((), ())), preferred_element_type=F32)


def _rmsnorm(x, g):
    ms = jnp.mean(x * x, axis=-1, keepdims=True)
    return x * lax.rsqrt(ms + RMS_EPS) * g


def _head_ones():
    r = lax.broadcasted_iota(jnp.int32, (MXU_TILE, MXU_TILE), 0) // HEAD
    c = lax.broadcasted_iota(jnp.int32, (MXU_TILE, MXU_TILE), 1) // HEAD
    return jnp.where(r == c, 1.0, 0.0).astype(BF16)


def _seg_sum(x, ones_bd):
    hi = x.astype(BF16)
    r1 = x - hi.astype(F32)
    mid = r1.astype(BF16)
    lo = (r1 - mid.astype(F32)).astype(BF16)
    outs = []
    for c in range(x.shape[-1] // MXU_TILE):
        sl = slice(c * MXU_TILE, (c + 1) * MXU_TILE)
        s = jnp.dot(hi[:, sl], ones_bd, preferred_element_type=F32)
        s = s + jnp.dot(mid[:, sl], ones_bd, preferred_element_type=F32)
        s = s + jnp.dot(lo[:, sl], ones_bd, preferred_element_type=F32)
        outs.append(s)
    return jnp.concatenate(outs, axis=-1)


def _const_spec(shape):
    nd = len(shape)
    return pl.BlockSpec(shape, lambda *_: (0,) * nd)


def _params(sem):
    return pltpu.CompilerParams(dimension_semantics=sem, vmem_limit_bytes=VMEM_LIMIT)


def _conv_pre(x, g, w1, b1):
    d = x.shape[-1]
    u = _mm(_rmsnorm(x, g), w1) + b1
    return u[:, :d] * jax.nn.sigmoid(u[:, d:])


def _conv_post(x, c, lng, lnb, w2, b2):
    mu = jnp.mean(c, axis=-1, keepdims=True)
    cc = c - mu
    var = jnp.mean(cc * cc, axis=-1, keepdims=True)
    cn = cc * lax.rsqrt(var + LN_EPS) * lng + lnb
    return x + _mm(cn * jax.nn.sigmoid(cn), w2) + b2


def _conv_seq_kernel(x_ref, buf0_ref, g_ref, w1_ref, b1_ref, wdw_ref, bdw_ref, lng_ref, lnb_ref,
                     w2_ref, b2_ref, h_ref, nb_ref, ubuf, cbuf, *, tm):
    t = pl.program_id(1)
    d = x_ref.shape[-1]

    @pl.when(t == 0)
    def _():
        ubuf[0:BUF_PAD, :] = buf0_ref[0]

    x = x_ref[0]
    ubuf[BUF_PAD:BUF_PAD + tm, :] = _conv_pre(x, g_ref[...], w1_ref[...], b1_ref[...])
    off = BUF_PAD - CONV_BUF
    for l in range(d // LANE):
        ls = slice(l * LANE, (l + 1) * LANE)
        acc = jnp.broadcast_to(bdw_ref[:, ls], (tm, LANE))
        for j in range(CONV_W):
            acc = acc + wdw_ref[j:j + 1, ls] * ubuf[off + j:off + j + tm, ls]
        cbuf[:, ls] = acc
    h_ref[0] = _conv_post(x, cbuf[...], lng_ref[...], lnb_ref[...], w2_ref[...], b2_ref[...])
    tail = ubuf[tm:tm + BUF_PAD, :]
    ubuf[0:BUF_PAD, :] = tail

    @pl.when(t == pl.num_programs(1) - 1)
    def _():
        nb_ref[0] = tail


def _conv_seq(x, buf0, g, w1, b1, wdw, bdw, lng, lnb, w2, b2, *, tm):
    B, T, D = x.shape
    nt = T // tm
    shared = buf0.shape[0] == 1
    buf_map = (lambda b, t: (0, 0, 0)) if shared else (lambda b, t: (b, 0, 0))
    return pl.pallas_call(
        functools.partial(_conv_seq_kernel, tm=tm),
        grid=(B, nt),
        in_specs=[
            pl.BlockSpec((1, tm, D), lambda b, t: (b, t, 0)),
            pl.BlockSpec((1, BUF_PAD, D), buf_map),
            _const_spec((1, D)), _const_spec((D, 2 * D)), _const_spec((1, 2 * D)),
            _const_spec((CONV_W, D)), _const_spec((1, D)), _const_spec((1, D)), _const_spec((1, D)),
            _const_spec((D, D)), _const_spec((1, D)),
        ],
        out_specs=[
            pl.BlockSpec((1, tm, D), lambda b, t: (b, t, 0)),
            pl.BlockSpec((1, BUF_PAD, D), lambda b, t: (b, 0, 0)),
        ],
        out_shape=[jax.ShapeDtypeStruct((B, T, D), F32), jax.ShapeDtypeStruct((B, BUF_PAD, D), F32)],
        scratch_shapes=[pltpu.VMEM((BUF_PAD + tm, D), F32), pltpu.VMEM((tm, D), F32)],
        compiler_params=_params(("arbitrary", "arbitrary")),
        name="conv_seq",
    )(x, buf0, g, w1, b1, wdw, bdw, lng, lnb, w2, b2)


def _conv_step_kernel(x_ref, buft_ref, g_ref, w1_ref, b1_ref, wdw_ref, bdw_ref, lng_ref, lnb_ref,
                      w2_ref, b2_ref, h_ref, u_ref, cbuf):
    x = x_ref[...]
    rows, d = x.shape
    u_ref[...] = _conv_pre(x, g_ref[...], w1_ref[...], b1_ref[...])
    for l in range(d // LANE):
        ls = slice(l * LANE, (l + 1) * LANE)
        acc = bdw_ref[:, ls] + wdw_ref[CONV_BUF:CONV_W, ls] * u_ref[:, ls]
        for j in range(CONV_BUF):
            acc = acc + wdw_ref[j:j + 1, ls] * buft_ref[j, :, ls]
        cbuf[:, ls] = acc
    h_ref[...] = _conv_post(x, cbuf[...], lng_ref[...], lnb_ref[...], w2_ref[...], b2_ref[...])


def _conv_step(x, buft, g, w1, b1, wdw, bdw, lng, lnb, w2, b2):
    R, D = x.shape
    return pl.pallas_call(
        _conv_step_kernel,
        grid=(1,),
        in_specs=[
            _const_spec((R, D)), _const_spec((CONV_BUF, R, D)),
            _const_spec((1, D)), _const_spec((D, 2 * D)), _const_spec((1, 2 * D)),
            _const_spec((CONV_W, D)), _const_spec((1, D)), _const_spec((1, D)), _const_spec((1, D)),
            _const_spec((D, D)), _const_spec((1, D)),
        ],
        out_specs=[_const_spec((R, D)), _const_spec((R, D))],
        out_shape=[jax.ShapeDtypeStruct((R, D), F32), jax.ShapeDtypeStruct((R, D), F32)],
        scratch_shapes=[pltpu.VMEM((R, D), F32)],
        compiler_params=_params(("arbitrary",)),
        name="conv_step",
    )(x, buft, g, w1, b1, wdw, bdw, lng, lnb, w2, b2)


def _mlp_kernel(x_ref, g_ref, win_ref, wout_ref, gf_ref, o_ref, *, final):
    x = x_ref[...]
    hid = jnp.maximum(_mm(_rmsnorm(x, g_ref[...]), win_ref[...]), 0.0)
    y = x + _mm(hid * hid, wout_ref[...])
    if final:
        y = _rmsnorm(y, gf_ref[...])
    o_ref[...] = y


def _mlp(x, g, win, wout, gf, *, final, tm):
    R, D = x.shape
    F = win.shape[1]
    return pl.pallas_call(
        functools.partial(_mlp_kernel, final=final),
        grid=(R // tm,),
        in_specs=[
            pl.BlockSpec((tm, D), lambda i: (i, 0)),
            _const_spec((1, D)), _const_spec((D, F)), _const_spec((F, D)), _const_spec((1, D)),
        ],
        out_specs=pl.BlockSpec((tm, D), lambda i: (i, 0)),
        out_shape=jax.ShapeDtypeStruct((R, D), F32),
        compiler_params=_params(("arbitrary",)),
        name="mlp",
    )(x, g, win, wout, gf)


def _rwkv_pre_math(hn, prev, xmix, wr, wk, wv, w0, w1, w2, a0, a1, a2, g1, g2, kk_w, ka_w):
    xx = prev - hn
    xr = hn + xx * xmix[0:1]
    xw = hn + xx * xmix[1:2]
    xk = hn + xx * xmix[2:3]
    xv = hn + xx * xmix[3:4]
    xa = hn + xx * xmix[4:5]
    xg = hn + xx * xmix[5:6]
    r = _mm(xr, wr)
    k = _mm(xk, wk)
    v = _mm(xv, wv)
    z = w0 + _mm(jnp.tanh(_mm(xw, w1)), w2)
    w_log = -(jnp.maximum(-z, 0.0) + jnp.log(1.0 + jnp.exp(-jnp.abs(z)))) - 0.5
    lw = -jnp.exp(w_log)
    iclr = jax.nn.sigmoid(a0 + _mm(_mm(xa, a1), a2))
    gate = _mm(jax.nn.sigmoid(_mm(xg, g1)), g2)
    kk = k * kk_w
    nrm = jnp.sqrt(_seg_sum(kk * kk, _head_ones()))
    kk = kk / jnp.maximum(nrm, L2_EPS)
    k2 = k * (1.0 + (iclr - 1.0) * ka_w)
    return r, lw, k2, v, -kk, kk * iclr, gate


def _rwkv_pre_seq_kernel(h_ref, sh0_ref, gn_ref, xmix_ref, wr_ref, wk_ref, wv_ref, w0_ref, w1_ref, w2_ref,
                         a0_ref, a1_ref, a2_ref, g1_ref, g2_ref, kk_ref, ka_ref,
                         r_ref, lw_ref, k_ref, v_ref, a_ref, b_ref, g_ref, sh_ref, carry, *, tm):
    t = pl.program_id(1)

    @pl.when(t == 0)
    def _():
        carry[...] = sh0_ref[0]

    hn = _rmsnorm(h_ref[0], gn_ref[...])
    row = lax.broadcasted_iota(jnp.int32, hn.shape, 0)
    prev = jnp.where(row == 0, carry[...], pltpu.roll(hn, 1, 0))
    last = hn[tm - 1:tm, :]
    carry[...] = last
    outs = _rwkv_pre_math(hn, prev, xmix_ref[...], wr_ref[...], wk_ref[...], wv_ref[...], w0_ref[...],
                          w1_ref[...], w2_ref[...], a0_ref[...], a1_ref[...], a2_ref[...], g1_ref[...],
                          g2_ref[...], kk_ref[...], ka_ref[...])
    for o_ref, o in zip((r_ref, lw_ref, k_ref, v_ref, a_ref, b_ref, g_ref), outs):
        o_ref[0] = o

    @pl.when(t == pl.num_programs(1) - 1)
    def _():
        sh_ref[0] = last


def _rwkv_weight_specs(D, p):
    return [
        _const_spec((1, D)), _const_spec((6, D)),
        _const_spec((D, D)), _const_spec((D, D)), _const_spec((D, D)),
        _const_spec((1, D)), _const_spec(p["w1"].shape), _const_spec(p["w2"].shape),
        _const_spec((1, D)), _const_spec(p["a1"].shape), _const_spec(p["a2"].shape),
        _const_spec(p["g1"].shape), _const_spec(p["g2"].shape),
        _const_spec((1, D)), _const_spec((1, D)),
    ]


def _rwkv_weight_args(p):
    return (p["gn"], p["xmix"], p["wr"], p["wk"], p["wv"], p["w0"], p["w1"], p["w2"],
            p["a0"], p["a1"], p["a2"], p["g1"], p["g2"], p["kk"], p["ka"])


def _rwkv_pre_seq(h, sh0, p, *, tm):
    B, T, D = h.shape
    shared = sh0.shape[0] == 1
    sh_map = (lambda b, t: (0, 0, 0)) if shared else (lambda b, t: (b, 0, 0))
    tile = pl.BlockSpec((1, tm, D), lambda b, t: (b, t, 0))
    return pl.pallas_call(
        functools.partial(_rwkv_pre_seq_kernel, tm=tm),
        grid=(B, T // tm),
        in_specs=[tile, pl.BlockSpec((1, 1, D), sh_map)] + _rwkv_weight_specs(D, p),
        out_specs=[tile] * 7 + [pl.BlockSpec((1, 1, D), lambda b, t: (b, 0, 0))],
        out_shape=[jax.ShapeDtypeStruct((B, T, D), F32)] * 7 + [jax.ShapeDtypeStruct((B, 1, D), F32)],
        scratch_shapes=[pltpu.VMEM((1, D), F32)],
        compiler_params=_params(("arbitrary", "arbitrary")),
        name="rwkv_pre_seq",
    )(h, sh0, *_rwkv_weight_args(p))


def _rwkv_pre_step_kernel(h_ref, prev_ref, gn_ref, xmix_ref, wr_ref, wk_ref, wv_ref, w0_ref, w1_ref, w2_ref,
                          a0_ref, a1_ref, a2_ref, g1_ref, g2_ref, kk_ref, ka_ref,
                          r_ref, lw_ref, k_ref, v_ref, a_ref, b_ref, g_ref, sh_ref):
    hn = _rmsnorm(h_ref[...], gn_ref[...])
    outs = _rwkv_pre_math(hn, prev_ref[...], xmix_ref[...], wr_ref[...], wk_ref[...], wv_ref[...], w0_ref[...],
                          w1_ref[...], w2_ref[...], a0_ref[...], a1_ref[...], a2_ref[...], g1_ref[...],
                          g2_ref[...], kk_ref[...], ka_ref[...])
    for o_ref, o in zip((r_ref, lw_ref, k_ref, v_ref, a_ref, b_ref, g_ref), outs):
        o_ref[...] = o
    sh_ref[...] = hn


def _rwkv_pre_step(h, prev, p):
    R, D = h.shape
    full = _const_spec((R, D))
    return pl.pallas_call(
        _rwkv_pre_step_kernel,
        grid=(1,),
        in_specs=[full, full] + _rwkv_weight_specs(D, p),
        out_specs=[full] * 8,
        out_shape=[jax.ShapeDtypeStruct((R, D), F32)] * 8,
        compiler_params=_params(("arbitrary",)),
        name="rwkv_pre_step",
    )(h, prev, *_rwkv_weight_args(p))


def _wkv_seq_kernel(r_ref, lw_ref, k_ref, v_ref, a_ref, b_ref, s0_ref, y_ref, sout_ref, state, *, L):
    c = pl.program_id(1)
    nh = state.shape[0]

    @pl.when(c == 0)
    def _():
        state[...] = s0_ref[0]

    lw = lw_ref[0]
    row = lax.broadcasted_iota(jnp.int32, lw.shape, 0)
    cs = lw
    s = 1
    while s < L:
        cs = cs + jnp.where(row >= s, pltpu.roll(cs, s, 0), 0.0)
        s *= 2
    gam = jnp.exp(cs)
    ginv = jnp.exp(-cs)
    rt = (r_ref[0] * gam).astype(BF16)
    at = (a_ref[0] * jnp.exp(cs - lw)).astype(BF16)
    bt = (b_ref[0] * ginv).astype(BF16)
    kt = (k_ref[0] * ginv).astype(BF16)
    vb = v_ref[0].astype(BF16)
    g_last = gam[L - 1:L, :]

    ri = lax.broadcasted_iota(jnp.int32, (L, L), 0)
    ci = lax.broadcasted_iota(jnp.int32, (L, L), 1)
    strict = ri > ci
    incl = ri >= ci
    eye = jnp.where(ri == ci, 1.0, 0.0)

    for h in range(nh):
        sl = slice(h * HEAD, (h + 1) * HEAD)
        A, Bm, K, R, V = at[:, sl], bt[:, sl], kt[:, sl], rt[:, sl], vb[:, sl]
        m_ab = jnp.where(strict, _dot_nt(A, Bm), 0.0)
        m_ak = jnp.where(strict, _dot_nt(A, K), 0.0)
        m_rb = jnp.where(incl, _dot_nt(R, Bm), 0.0)
        m_rk = jnp.where(incl, _dot_nt(R, K), 0.0)
        n = m_ab
        tinv = eye + n
        p = 2
        while p < L:
            nb = n.astype(BF16)
            n = jnp.dot(nb, nb, preferred_element_type=F32)
            tinv = tinv + jnp.dot(tinv.astype(BF16), n.astype(BF16), preferred_element_type=F32)
            p *= 2
        S = state[h]
        Sb = S.astype(BF16)
        rhs = _dot_nt(A, Sb) + jnp.dot(m_ak.astype(BF16), V, preferred_element_type=F32)
        U = jnp.dot(tinv.astype(BF16), rhs.astype(BF16), preferred_element_type=F32)
        Ub = U.astype(BF16)
        Y = (_dot_nt(R, Sb) + jnp.dot(m_rb.astype(BF16), Ub, preferred_element_type=F32)
             + jnp.dot(m_rk.astype(BF16), V, preferred_element_type=F32))
        y_ref[0, :, sl] = Y
        state[h] = (S + _dot_tn(Ub, Bm) + _dot_tn(V, K)) * g_last[:, sl]

    @pl.when(c == pl.num_programs(1) - 1)
    def _():
        sout_ref[0] = state[...]


def _wkv_seq(r, lw, k, v, a, b, s0, *, L):
    B, T, D = r.shape
    nh = D // HEAD
    shared = s0.shape[0] == 1
    s_map = (lambda b_, c: (0, 0, 0, 0)) if shared else (lambda b_, c: (b_, 0, 0, 0))
    tile = pl.BlockSpec((1, L, D), lambda b_, c: (b_, c, 0))
    return pl.pallas_call(
        functools.partial(_wkv_seq_kernel, L=L),
        grid=(B, T // L),
        in_specs=[tile] * 6 + [pl.BlockSpec((1, nh, HEAD, HEAD), s_map)],
        out_specs=[tile, pl.BlockSpec((1, nh, HEAD, HEAD), lambda b_, c: (b_, 0, 0, 0))],
        out_shape=[jax.ShapeDtypeStruct((B, T, D), F32), jax.ShapeDtypeStruct((B, nh, HEAD, HEAD), F32)],
        scratch_shapes=[pltpu.VMEM((nh, HEAD, HEAD), F32)],
        compiler_params=_params(("arbitrary", "arbitrary")),
        name="wkv_seq",
    )(r, lw, k, v, a, b, s0)


def _wkv_step_kernel(r_ref, lw_ref, k_ref, v_ref, a_ref, b_ref, s_ref, y_ref, sout_ref):
    S = s_ref[...]
    sa = jnp.sum(S * a_ref[...], axis=-1, keepdims=True)
    S = S * jnp.exp(lw_ref[...]) + sa * b_ref[...] + v_ref[...] * k_ref[...]
    sout_ref[...] = S
    y_ref[...] = jnp.sum(S * r_ref[...], axis=-1, keepdims=True)


def _wkv_step(r, lw, k, v, a, b, s, *, bt):
    R, D = r.shape
    nh = D // HEAD
    rowv = lambda x: x.reshape(R, nh, 1, HEAD)
    row_spec = pl.BlockSpec((bt, nh, 1, HEAD), lambda i: (i, 0, 0, 0))
    col_spec = pl.BlockSpec((bt, nh, HEAD, 1), lambda i: (i, 0, 0, 0))
    st_spec = pl.BlockSpec((bt, nh, HEAD, HEAD), lambda i: (i, 0, 0, 0))
    y, s_new = pl.pallas_call(
        _wkv_step_kernel,
        grid=(R // bt,),
        in_specs=[row_spec, row_spec, row_spec, col_spec, row_spec, row_spec, st_spec],
        out_specs=[col_spec, st_spec],
        out_shape=[jax.ShapeDtypeStruct((R, nh, HEAD, 1), F32), jax.ShapeDtypeStruct((R, nh, HEAD, HEAD), F32)],
        compiler_params=_params(("arbitrary",)),
        name="wkv_step",
    )(rowv(r), rowv(lw), rowv(k), v.reshape(R, nh, HEAD, 1), rowv(a), rowv(b), s)
    return y.reshape(R, D), s_new


def _rwkv_post_kernel(y_ref, r_ref, k_ref, v_ref, g_ref, h_ref, gng_ref, gnb_ref, rk_ref, wo_ref, o_ref):
    ones_bd = _head_ones()
    y = y_ref[...]
    inv_n = 1.0 / HEAD
    mu = _seg_sum(y, ones_bd) * inv_n
    yc = y - mu
    var = _seg_sum(yc * yc, ones_bd) * inv_n
    yn = yc * lax.rsqrt(var + GN_EPS) * gng_ref[...] + gnb_ref[...]
    bonus = _seg_sum(r_ref[...] * k_ref[...] * rk_ref[...], ones_bd) * v_ref[...]
    o_ref[...] = h_ref[...] + _mm((yn + bonus) * g_ref[...], wo_ref[...])


def _rwkv_post(y, r, k, v, g, h, gng, gnb, rk, wo, *, tm):
    R, D = y.shape
    tile = pl.BlockSpec((tm, D), lambda i: (i, 0))
    return pl.pallas_call(
        _rwkv_post_kernel,
        grid=(R // tm,),
        in_specs=[tile] * 6 + [_const_spec((1, D))] * 3 + [_const_spec((D, D))],
        out_specs=tile,
        out_shape=jax.ShapeDtypeStruct((R, D), F32),
        compiler_params=_params(("arbitrary",)),
        name="rwkv_post",
    )(y, r, k, v, g, h, gng, gnb, rk, wo)


def _row_tile(rows, target):
    tm = min(rows, target)
    while rows % tm:
        tm //= 2
    return tm


def kernel(x_prompt, x_sample, state_conv, state_shift, state_wkv, meta_tokens, norm_mix, norm_mlp, norm_final, conv_w_pw1, conv_b_pw1, conv_w_dw, conv_b_dw, conv_ln_g, conv_ln_b, conv_w_pw2, conv_b_pw2, rwkv_x_mix, rwkv_w_r, rwkv_w_k, rwkv_w_v, rwkv_w_o, rwkv_w0, rwkv_w1, rwkv_w2, rwkv_a0, rwkv_a1, rwkv_a2, rwkv_g1, rwkv_g2, rwkv_k_k, rwkv_k_a, rwkv_r_k, rwkv_gn_g, rwkv_gn_b, w_mlp_in, w_mlp_out):
    B, T, D = x_prompt.shape
    SB = x_sample.shape[0]
    n_meta = meta_tokens.shape[0]
    nh = D // HEAD
    depth = norm_mix.shape[0]
    assert depth == 2 and x_sample.shape[1] == 1 and D % MXU_TILE == 0

    row = lambda x: x.reshape(1, -1).astype(F32)
    bf = lambda x: x.astype(BF16)

    conv_p = (row(norm_mix[0]), bf(conv_w_pw1[0]), row(conv_b_pw1[0]), conv_w_dw[0], row(conv_b_dw[0]),
              row(conv_ln_g[0]), row(conv_ln_b[0]), bf(conv_w_pw2[0]), row(conv_b_pw2[0]))
    rwkv_p = dict(gn=row(norm_mix[1]), xmix=rwkv_x_mix[0], wr=bf(rwkv_w_r[0]), wk=bf(rwkv_w_k[0]),
                  wv=bf(rwkv_w_v[0]), w0=row(rwkv_w0[0]), w1=bf(rwkv_w1[0]), w2=bf(rwkv_w2[0]),
                  a0=row(rwkv_a0[0]), a1=bf(rwkv_a1[0]), a2=bf(rwkv_a2[0]), g1=bf(rwkv_g1[0]),
                  g2=bf(rwkv_g2[0]), kk=row(rwkv_k_k[0]), ka=row(rwkv_k_a[0]))
    post_p = (row(rwkv_gn_g[0]), row(rwkv_gn_b[0]), row(rwkv_r_k[0]), bf(rwkv_w_o[0]))
    mlp_p = [(row(norm_mlp[i]), bf(w_mlp_in[i]), bf(w_mlp_out[i])) for i in range(depth)]
    gf = row(norm_final)

    def mlp(x2d, i, final):
        return _mlp(x2d, *mlp_p[i], gf, final=final, tm=_row_tile(x2d.shape[0], 256))

    def seq_trunk(x, buf0, sh0, s0, need_out):
        b, t, _ = x.shape
        h, nb = _conv_seq(x, buf0, *conv_p, tm=_row_tile(t, 256))
        h = mlp(h.reshape(b * t, D), 0, False).reshape(b, t, D)
        r, lw, k, v, a, bb, g, sh = _rwkv_pre_seq(h, sh0, rwkv_p, tm=_row_tile(t, 256))
        y, s1 = _wkv_seq(r, lw, k, v, a, bb, s0, L=_row_tile(t, 64))
        if not need_out:
            return None, nb, sh, s1
        f2 = lambda z: z.reshape(b * t, D)
        h = _rwkv_post(f2(y), f2(r), f2(k), f2(v), f2(g), f2(h), *post_p, tm=_row_tile(b * t, 256))
        return mlp(h, 1, True).reshape(b, t, D), nb, sh, s1

    zero_buf = jnp.zeros((1, BUF_PAD, D), F32)
    zero_sh = jnp.zeros((1, 1, D), F32)
    zero_s = jnp.zeros((1, nh, HEAD, HEAD), F32)
    _, m_buf, m_sh, m_s = seq_trunk(meta_tokens.astype(F32)[None], zero_buf, zero_sh, zero_s, False)

    y_prompt, p_buf, p_sh, p_s = seq_trunk(x_prompt, m_buf, m_sh, m_s, True)
    conv_prompt = p_buf[:, BUF_PAD - CONV_BUF:][None]
    shift_prompt = p_sh.reshape(1, B, D)
    wkv_prompt = p_s[None].astype(state_wkv.dtype)

    xs = x_sample.reshape(SB, D)
    buft = jnp.swapaxes(state_conv[0], 0, 1)
    h, u = _conv_step(xs, buft, *conv_p)
    conv_sample = jnp.concatenate([state_conv[0][:, 1:], u[:, None, :]], axis=1)[None]
    h = mlp(h, 0, False)
    r, lw, k, v, a, bb, g, sh = _rwkv_pre_step(h, state_shift[0], rwkv_p)
    y, s_new = _wkv_step(r, lw, k, v, a, bb, state_wkv[0].astype(F32), bt=8)
    h = _rwkv_post(y, r, k, v, g, h, *post_p, tm=SB)
    y_sample = mlp(h, 1, True).reshape(SB, 1, D)
    shift_sample = sh[None]
    wkv_sample = s_new[None].astype(state_wkv.dtype)

    return (y_prompt, y_sample, conv_prompt, shift_prompt, wkv_prompt,
            conv_sample, shift_sample, wkv_sample)
```

```python
import functools
import math

import jax
import jax.numpy as jnp
from jax import lax
from jax.experimental import pallas as pl
from jax.experimental.pallas import tpu as pltpu

F32 = jnp.float32
BF16 = jnp.bfloat16

RMS_EPS = 1e-6
LN_EPS = 1e-5
GN_EPS = 64e-5
L2_EPS = 1e-12
HEAD = 64
CONV_W = 31
CONV_BUF = CONV_W - 1
BUF_PAD = 32
MXU_TILE = 256
LANE = 128
VMEM_LIMIT = 52 * 1024 * 1024


def _mm(a, w):
    return jnp.dot(a.astype(BF16), w, preferred_element_type=F32)


def _dot_nt(a, b):
    return lax.dot_general(a, b, (((1,), (1,)), ((), ())), preferred_element_type=F32)


def _dot_tn(a, b):
    return lax.dot_general(a, b, (((0,), (0,)), ((), ())), preferred_element_type=F32)


def _rmsnorm(x, g):
    ms = jnp.mean(x * x, axis=-1, keepdims=True)
    return x * lax.rsqrt(ms + RMS_EPS) * g


def _head_ones():
    r = lax.broadcasted_iota(jnp.int32, (MXU_TILE, MXU_TILE), 0) // HEAD
    c = lax.broadcasted_iota(jnp.int32, (MXU_TILE, MXU_TILE), 1) // HEAD
    return jnp.where(r == c, 1.0, 0.0).astype(BF16)


def _seg_sum(x, ones_bd):
    hi = x.astype(BF16)
    r1 = x - hi.astype(F32)
    mid = r1.astype(BF16)
    lo = (r1 - mid.astype(F32)).astype(BF16)
    outs = []
    for c in range(x.shape[-1] // MXU_TILE):
        sl = slice(c * MXU_TILE, (c + 1) * MXU_TILE)
        s = jnp.dot(hi[:, sl], ones_bd, preferred_element_type=F32)
        s = s + jnp.dot(mid[:, sl], ones_bd, preferred_element_type=F32)
        s = s + jnp.dot(lo[:, sl], ones_bd, preferred_element_type=F32)
        outs.append(s)
    return jnp.concatenate(outs, axis=-1)


def _const_spec(shape):
    nd = len(shape)
    return pl.BlockSpec(shape, lambda *_: (0,) * nd)


def _params(sem):
    return pltpu.CompilerParams(dimension_semantics=sem, vmem_limit_bytes=VMEM_LIMIT)


def _conv_pre(x, g, w1, b1):
    d = x.shape[-1]
    u = _mm(_rmsnorm(x, g), w1) + b1
    return u[:, :d] * jax.nn.sigmoid(u[:, d:])


def _conv_post(x, c, lng, lnb, w2, b2):
    mu = jnp.mean(c, axis=-1, keepdims=True)
    cc = c - mu
    var = jnp.mean(cc * cc, axis=-1, keepdims=True)
    cn = cc * lax.rsqrt(var + LN_EPS) * lng + lnb
    return x + _mm(cn * jax.nn.sigmoid(cn), w2) + b2


def _conv_seq_kernel(x_ref, buf0_ref, g_ref, w1_ref, b1_ref, wdw_ref, bdw_ref, lng_ref, lnb_ref,
                     w2_ref, b2_ref, h_ref, nb_ref, ubuf, cbuf, *, tm):
    t = pl.program_id(1)
    d = x_ref.shape[-1]

    @pl.when(t == 0)
    def _():
        ubuf[0:BUF_PAD, :] = buf0_ref[0]

    x = x_ref[0]
    ubuf[BUF_PAD:BUF_PAD + tm, :] = _conv_pre(x, g_ref[...], w1_ref[...], b1_ref[...])
    off = BUF_PAD - CONV_BUF
    for l in range(d // LANE):
        ls = slice(l * LANE, (l + 1) * LANE)
        acc = jnp.broadcast_to(bdw_ref[:, ls], (tm, LANE))
        for j in range(CONV_W):
            acc = acc + wdw_ref[j:j + 1, ls] * ubuf[off + j:off + j + tm, ls]
        cbuf[:, ls] = acc
    h_ref[0] = _conv_post(x, cbuf[...], lng_ref[...], lnb_ref[...], w2_ref[...], b2_ref[...])
    tail = ubuf[tm:tm + BUF_PAD, :]
    ubuf[0:BUF_PAD, :] = tail

    @pl.when(t == pl.num_programs(1) - 1)
    def _():
        nb_ref[0] = tail


def _conv_seq(x, buf0, g, w1, b1, wdw, bdw, lng, lnb, w2, b2, *, tm):
    B, T, D = x.shape
    nt = T // tm
    shared = buf0.shape[0] == 1
    buf_map = (lambda b, t: (0, 0, 0)) if shared else (lambda b, t: (b, 0, 0))
    return pl.pallas_call(
        functools.partial(_conv_seq_kernel, tm=tm),
        grid=(B, nt),
        in_specs=[
            pl.BlockSpec((1, tm, D), lambda b, t: (b, t, 0)),
            pl.BlockSpec((1, BUF_PAD, D), buf_map),
            _const_spec((1, D)), _const_spec((D, 2 * D)), _const_spec((1, 2 * D)),
            _const_spec((CONV_W, D)), _const_spec((1, D)), _const_spec((1, D)), _const_spec((1, D)),
            _const_spec((D, D)), _const_spec((1, D)),
        ],
        out_specs=[
            pl.BlockSpec((1, tm, D), lambda b, t: (b, t, 0)),
            pl.BlockSpec((1, BUF_PAD, D), lambda b, t: (b, 0, 0)),
        ],
        out_shape=[jax.ShapeDtypeStruct((B, T, D), F32), jax.ShapeDtypeStruct((B, BUF_PAD, D), F32)],
        scratch_shapes=[pltpu.VMEM((BUF_PAD + tm, D), F32), pltpu.VMEM((tm, D), F32)],
        compiler_params=_params(("arbitrary", "arbitrary")),
        name="conv_seq",
    )(x, buf0, g, w1, b1, wdw, bdw, lng, lnb, w2, b2)


def _conv_step_kernel(x_ref, buft_ref, g_ref, w1_ref, b1_ref, wdw_ref, bdw_ref, lng_ref, lnb_ref,
                      w2_ref, b2_ref, h_ref, u_ref, cbuf):
    x = x_ref[...]
    rows, d = x.shape
    u_ref[...] = _conv_pre(x, g_ref[...], w1_ref[...], b1_ref[...])
    for l in range(d // LANE):
        ls = slice(l * LANE, (l + 1) * LANE)
        acc = bdw_ref[:, ls] + wdw_ref[CONV_BUF:CONV_W, ls] * u_ref[:, ls]
        for j in range(CONV_BUF):
            acc = acc + wdw_ref[j:j + 1, ls] * buft_ref[j, :, ls]
        cbuf[:, ls] = acc
    h_ref[...] = _conv_post(x, cbuf[...], lng_ref[...], lnb_ref[...], w2_ref[...], b2_ref[...])


def _conv_step(x, buft, g, w1, b1, wdw, bdw, lng, lnb, w2, b2):
    R, D = x.shape
    return pl.pallas_call(
        _conv_step_kernel,
        grid=(1,),
        in_specs=[
            _const_spec((R, D)), _const_spec((CONV_BUF, R, D)),
            _const_spec((1, D)), _const_spec((D, 2 * D)), _const_spec((1, 2 * D)),
            _const_spec((CONV_W, D)), _const_spec((1, D)), _const_spec((1, D)), _const_spec((1, D)),
            _const_spec((D, D)), _const_spec((1, D)),
        ],
        out_specs=[_const_spec((R, D)), _const_spec((R, D))],
        out_shape=[jax.ShapeDtypeStruct((R, D), F32), jax.ShapeDtypeStruct((R, D), F32)],
        scratch_shapes=[pltpu.VMEM((R, D), F32)],
        compiler_params=_params(("arbitrary",)),
        name="conv_step",
    )(x, buft, g, w1, b1, wdw, bdw, lng, lnb, w2, b2)


def _mlp_kernel(x_ref, g_ref, win_ref, wout_ref, gf_ref, o_ref, *, final):
    x = x_ref[...]
    hid = jnp.maximum(_mm(_rmsnorm(x, g_ref[...]), win_ref[...]), 0.0)
    y = x + _mm(hid * hid, wout_ref[...])
    if final:
        y = _rmsnorm(y, gf_ref[...])
    o_ref[...] = y


def _mlp(x, g, win, wout, gf, *, final, tm):
    R, D = x.shape
    F = win.shape[1]
    return pl.pallas_call(
        functools.partial(_mlp_kernel, final=final),
        grid=(R // tm,),
        in_specs=[
            pl.BlockSpec((tm, D), lambda i: (i, 0)),
            _const_spec((1, D)), _const_spec((D, F)), _const_spec((F, D)), _const_spec((1, D)),
        ],
        out_specs=pl.BlockSpec((tm, D), lambda i: (i, 0)),
        out_shape=jax.ShapeDtypeStruct((R, D), F32),
        compiler_params=_params(("arbitrary",)),
        name="mlp",
    )(x, g, win, wout, gf)


def _rwkv_pre_math(hn, prev, xmix, wr, wk, wv, w0, w1, w2, a0, a1, a2, g1, g2, kk_w, ka_w):
    xx = prev - hn
    xr = hn + xx * xmix[0:1]
    xw = hn + xx * xmix[1:2]
    xk = hn + xx * xmix[2:3]
    xv = hn + xx * xmix[3:4]
    xa = hn + xx * xmix[4:5]
    xg = hn + xx * xmix[5:6]
    r = _mm(xr, wr)
    k = _mm(xk, wk)
    v = _mm(xv, wv)
    z = w0 + _mm(jnp.tanh(_mm(xw, w1)), w2)
    w_log = -(jnp.maximum(-z, 0.0) + jnp.log(1.0 + jnp.exp(-jnp.abs(z)))) - 0.5
    lw = -jnp.exp(w_log)
    iclr = jax.nn.sigmoid(a0 + _mm(_mm(xa, a1), a2))
    gate = _mm(jax.nn.sigmoid(_mm(xg, g1)), g2)
    kk = k * kk_w
    nrm = jnp.sqrt(_seg_sum(kk * kk, _head_ones()))
    kk = kk / jnp.maximum(nrm, L2_EPS)
    k2 = k * (1.0 + (iclr - 1.0) * ka_w)
    return r, lw, k2, v, -kk, kk * iclr, gate


def _rwkv_pre_seq_kernel(h_ref, sh0_ref, gn_ref, xmix_ref, wr_ref, wk_ref, wv_ref, w0_ref, w1_ref, w2_ref,
                         a0_ref, a1_ref, a2_ref, g1_ref, g2_ref, kk_ref, ka_ref,
                         r_ref, lw_ref, k_ref, v_ref, a_ref, b_ref, g_ref, sh_ref, carry, *, tm):
    t = pl.program_id(1)

    @pl.when(t == 0)
    def _():
        carry[...] = sh0_ref[0]

    hn = _rmsnorm(h_ref[0], gn_ref[...])
    row = lax.broadcasted_iota(jnp.int32, hn.shape, 0)
    prev = jnp.where(row == 0, carry[...], pltpu.roll(hn, 1, 0))
    last = hn[tm - 1:tm, :]
    carry[...] = last
    outs = _rwkv_pre_math(hn, prev, xmix_ref[...], wr_ref[...], wk_ref[...], wv_ref[...], w0_ref[...],
                          w1_ref[...], w2_ref[...], a0_ref[...], a1_ref[...], a2_ref[...], g1_ref[...],
                          g2_ref[...], kk_ref[...], ka_ref[...])
    for o_ref, o in zip((r_ref, lw_ref, k_ref, v_ref, a_ref, b_ref, g_ref), outs):
        o_ref[0] = o

    @pl.when(t == pl.num_programs(1) - 1)
    def _():
        sh_ref[0] = last


def _rwkv_weight_specs(D, p):
    return [
        _const_spec((1, D)), _const_spec((6, D)),
        _const_spec((D, D)), _const_spec((D, D)), _const_spec((D, D)),
        _const_spec((1, D)), _const_spec(p["w1"].shape), _const_spec(p["w2"].shape),
        _const_spec((1, D)), _const_spec(p["a1"].shape), _const_spec(p["a2"].shape),
        _const_spec(p["g1"].shape), _const_spec(p["g2"].shape),
        _const_spec((1, D)), _const_spec((1, D)),
    ]


def _rwkv_weight_args(p):
    return (p["gn"], p["xmix"], p["wr"], p["wk"], p["wv"], p["w0"], p["w1"], p["w2"],
            p["a0"], p["a1"], p["a2"], p["g1"], p["g2"], p["kk"], p["ka"])


def _rwkv_pre_seq(h, sh0, p, *, tm):
    B, T, D = h.shape
    shared = sh0.shape[0] == 1
    sh_map = (lambda b, t: (0, 0, 0)) if shared else (lambda b, t: (b, 0, 0))
    tile = pl.BlockSpec((1, tm, D), lambda b, t: (b, t, 0))
    return pl.pallas_call(
        functools.partial(_rwkv_pre_seq_kernel, tm=tm),
        grid=(B, T // tm),
        in_specs=[tile, pl.BlockSpec((1, 1, D), sh_map)] + _rwkv_weight_specs(D, p),
        out_specs=[tile] * 7 + [pl.BlockSpec((1, 1, D), lambda b, t: (b, 0, 0))],
        out_shape=[jax.ShapeDtypeStruct((B, T, D), F32)] * 7 + [jax.ShapeDtypeStruct((B, 1, D), F32)],
        scratch_shapes=[pltpu.VMEM((1, D), F32)],
        compiler_params=_params(("arbitrary", "arbitrary")),
        name="rwkv_pre_seq",
    )(h, sh0, *_rwkv_weight_args(p))


def _rwkv_pre_step_kernel(h_ref, prev_ref, gn_ref, xmix_ref, wr_ref, wk_ref, wv_ref, w0_ref, w1_ref, w2_ref,
                          a0_ref, a1_ref, a2_ref, g1_ref, g2_ref, kk_ref, ka_ref,
                          r_ref, lw_ref, k_ref, v_ref, a_ref, b_ref, g_ref, sh_ref):
    hn = _rmsnorm(h_ref[...], gn_ref[...])
    outs = _rwkv_pre_math(hn, prev_ref[...], xmix_ref[...], wr_ref[...], wk_ref[...], wv_ref[...], w0_ref[...],
                          w1_ref[...], w2_ref[...], a0_ref[...], a1_ref[...], a2_ref[...], g1_ref[...],
                          g2_ref[...], kk_ref[...], ka_ref[...])
    for o_ref, o in zip((r_ref, lw_ref, k_ref, v_ref, a_ref, b_ref, g_ref), outs):
        o_ref[...] = o
    sh_ref[...] = hn


def _rwkv_pre_step(h, prev, p):
    R, D = h.shape
    full = _const_spec((R, D))
    return pl.pallas_call(
        _rwkv_pre_step_kernel,
        grid=(1,),
        in_specs=[full, full] + _rwkv_weight_specs(D, p),
        out_specs=[full] * 8,
        out_shape=[jax.ShapeDtypeStruct((R, D), F32)] * 8,
        compiler_params=_params(("arbitrary",)),
        name="rwkv_pre_step",
    )(h, prev, *_rwkv_weight_args(p))


def _wkv_seq_kernel(r_ref, lw_ref, k_ref, v_ref, a_ref, b_ref, s0_ref, y_ref, sout_ref, state, *, L):
    c = pl.program_id(1)
    nh = state.shape[0]

    @pl.when(c == 0)
    def _():
        state[...] = s0_ref[0]

    lw = lw_ref[0]
    row = lax.broadcasted_iota(jnp.int32, lw.shape, 0)
    cs = lw
    s = 1
    while s < L:
        cs = cs + jnp.where(row >= s, pltpu.roll(cs, s, 0), 0.0)
        s *= 2
    gam = jnp.exp(cs)
    ginv = jnp.exp(-cs)
    g_last = gam[L - 1:L, :]
    rt_f = r_ref[0] * gam
    bt_f = b_ref[0] * ginv
    kt_f = k_ref[0] * ginv
    rt = rt_f.astype(BF16)
    at = (a_ref[0] * jnp.exp(cs - lw)).astype(BF16)
    bt = bt_f.astype(BF16)
    kt = kt_f.astype(BF16)
    bg = (bt_f * g_last).astype(BF16)
    kg = (kt_f * g_last).astype(BF16)
    vb = v_ref[0].astype(BF16)

    ri = lax.broadcasted_iota(jnp.int32, (L, L), 0)
    ci = lax.broadcasted_iota(jnp.int32, (L, L), 1)
    strict = ri > ci
    incl = ri >= ci
    eye = jnp.where(ri == ci, 1.0, 0.0)
    heads = range(nh)
    hs = lambda x, h: x[:, h * HEAD:(h + 1) * HEAD]
    dot = lambda x, y: jnp.dot(x, y, preferred_element_type=F32)
    cat0 = lambda x, y: jnp.concatenate([x, y], axis=0)

    P = [_dot_nt(cat0(hs(at, h), hs(rt, h)), cat0(hs(bt, h), hs(kt, h))) for h in heads]
    n0 = [jnp.where(strict, p[:L, :L], 0.0) for p in P]
    mak = [jnp.where(strict, p[:L, L:], 0.0).astype(BF16) for p in P]
    mrb = [jnp.where(incl, p[L:, :L], 0.0).astype(BF16) for p in P]
    mrk = [jnp.where(incl, p[L:, L:], 0.0).astype(BF16) for p in P]
    MV = [dot(cat0(mak[h], mrk[h]), hs(vb, h)) for h in heads]
    pw = [n.astype(BF16) for n in n0]
    tcur = [eye + n for n in n0]
    pw = [dot(p, p).astype(BF16) for p in pw]
    for _ in range(2, int(math.log2(L))):
        Z = [dot(cat0(pw[h], tcur[h].astype(BF16)), pw[h]) for h in heads]
        pw = [z[:L].astype(BF16) for z in Z]
        tcur = [tcur[h] + Z[h][L:] for h in heads]
    tfin = [(tcur[h] + dot(tcur[h].astype(BF16), pw[h])).astype(BF16) for h in heads]
    TX = [dot(tfin[h], jnp.concatenate([hs(at, h), MV[h][:L].astype(BF16)], axis=1)).astype(BF16)
          for h in heads]
    RY = [dot(mrb[h], TX[h]) for h in heads]
    rbar = [(hs(rt_f, h) + RY[h][:, :HEAD]).astype(BF16) for h in heads]
    y0 = [RY[h][:, HEAD:] + MV[h][L:] for h in heads]
    GH = [_dot_tn(TX[h], hs(bg, h)) for h in heads]
    VK = [_dot_tn(hs(vb, h), hs(kg, h)) for h in heads]
    for h in heads:
        S = state[h]
        Sb = S.astype(BF16)
        y_ref[0, :, h * HEAD:(h + 1) * HEAD] = _dot_nt(rbar[h], Sb) + y0[h]
        state[h] = S * hs(g_last, h) + dot(Sb, GH[h][:HEAD].astype(BF16)) + GH[h][HEAD:] + VK[h]

    @pl.when(c == pl.num_programs(1) - 1)
    def _():
        sout_ref[0] = state[...]


def _wkv_seq(r, lw, k, v, a, b, s0, *, L):
    B, T, D = r.shape
    nh = D // HEAD
    shared = s0.shape[0] == 1
    s_map = (lambda b_, c: (0, 0, 0, 0)) if shared else (lambda b_, c: (b_, 0, 0, 0))
    tile = pl.BlockSpec((1, L, D), lambda b_, c: (b_, c, 0))
    return pl.pallas_call(
        functools.partial(_wkv_seq_kernel, L=L),
        grid=(B, T // L),
        in_specs=[tile] * 6 + [pl.BlockSpec((1, nh, HEAD, HEAD), s_map)],
        out_specs=[tile, pl.BlockSpec((1, nh, HEAD, HEAD), lambda b_, c: (b_, 0, 0, 0))],
        out_shape=[jax.ShapeDtypeStruct((B, T, D), F32), jax.ShapeDtypeStruct((B, nh, HEAD, HEAD), F32)],
        scratch_shapes=[pltpu.VMEM((nh, HEAD, HEAD), F32)],
        compiler_params=_params(("arbitrary", "arbitrary")),
        name="wkv_seq",
    )(r, lw, k, v, a, b, s0)


def _wkv_step_kernel(r_ref, lw_ref, k_ref, v_ref, a_ref, b_ref, s_ref, y_ref, sout_ref):
    S = s_ref[...]
    sa = jnp.sum(S * a_ref[...], axis=-1, keepdims=True)
    S = S * jnp.exp(lw_ref[...]) + sa * b_ref[...] + v_ref[...] * k_ref[...]
    sout_ref[...] = S
    y_ref[...] = jnp.sum(S * r_ref[...], axis=-1, keepdims=True)


def _wkv_step(r, lw, k, v, a, b, s, *, bt):
    R, D = r.shape
    nh = D // HEAD
    rowv = lambda x: x.reshape(R, nh, 1, HEAD)
    row_spec = pl.BlockSpec((bt, nh, 1, HEAD), lambda i: (i, 0, 0, 0))
    col_spec = pl.BlockSpec((bt, nh, HEAD, 1), lambda i: (i, 0, 0, 0))
    st_spec = pl.BlockSpec((bt, nh, HEAD, HEAD), lambda i: (i, 0, 0, 0))
    y, s_new = pl.pallas_call(
        _wkv_step_kernel,
        grid=(R // bt,),
        in_specs=[row_spec, row_spec, row_spec, col_spec, row_spec, row_spec, st_spec],
        out_specs=[col_spec, st_spec],
        out_shape=[jax.ShapeDtypeStruct((R, nh, HEAD, 1), F32), jax.ShapeDtypeStruct((R, nh, HEAD, HEAD), F32)],
        compiler_params=_params(("arbitrary",)),
        name="wkv_step",
    )(rowv(r), rowv(lw), rowv(k), v.reshape(R, nh, HEAD, 1), rowv(a), rowv(b), s)
    return y.reshape(R, D), s_new


def _rwkv_post_kernel(y_ref, r_ref, k_ref, v_ref, g_ref, h_ref, gng_ref, gnb_ref, rk_ref, wo_ref, o_ref):
    ones_bd = _head_ones()
    y = y_ref[...]
    inv_n = 1.0 / HEAD
    mu = _seg_sum(y, ones_bd) * inv_n
    yc = y - mu
    var = _seg_sum(yc * yc, ones_bd) * inv_n
    yn = yc * lax.rsqrt(var + GN_EPS) * gng_ref[...] + gnb_ref[...]
    bonus = _seg_sum(r_ref[...] * k_ref[...] * rk_ref[...], ones_bd) * v_ref[...]
    o_ref[...] = h_ref[...] + _mm((yn + bonus) * g_ref[...], wo_ref[...])


def _rwkv_post(y, r, k, v, g, h, gng, gnb, rk, wo, *, tm):
    R, D = y.shape
    tile = pl.BlockSpec((tm, D), lambda i: (i, 0))
    return pl.pallas_call(
        _rwkv_post_kernel,
        grid=(R // tm,),
        in_specs=[tile] * 6 + [_const_spec((1, D))] * 3 + [_const_spec((D, D))],
        out_specs=tile,
        out_shape=jax.ShapeDtypeStruct((R, D), F32),
        compiler_params=_params(("arbitrary",)),
        name="rwkv_post",
    )(y, r, k, v, g, h, gng, gnb, rk, wo)


def _row_tile(rows, target):
    tm = min(rows, target)
    while rows % tm:
        tm //= 2
    return tm


def kernel(x_prompt, x_sample, state_conv, state_shift, state_wkv, meta_tokens, norm_mix, norm_mlp, norm_final, conv_w_pw1, conv_b_pw1, conv_w_dw, conv_b_dw, conv_ln_g, conv_ln_b, conv_w_pw2, conv_b_pw2, rwkv_x_mix, rwkv_w_r, rwkv_w_k, rwkv_w_v, rwkv_w_o, rwkv_w0, rwkv_w1, rwkv_w2, rwkv_a0, rwkv_a1, rwkv_a2, rwkv_g1, rwkv_g2, rwkv_k_k, rwkv_k_a, rwkv_r_k, rwkv_gn_g, rwkv_gn_b, w_mlp_in, w_mlp_out):
    B, T, D = x_prompt.shape
    SB = x_sample.shape[0]
    n_meta = meta_tokens.shape[0]
    nh = D // HEAD
    depth = norm_mix.shape[0]
    assert depth == 2 and x_sample.shape[1] == 1 and D % MXU_TILE == 0

    row = lambda x: x.reshape(1, -1).astype(F32)
    bf = lambda x: x.astype(BF16)

    conv_p = (row(norm_mix[0]), bf(conv_w_pw1[0]), row(conv_b_pw1[0]), conv_w_dw[0], row(conv_b_dw[0]),
              row(conv_ln_g[0]), row(conv_ln_b[0]), bf(conv_w_pw2[0]), row(conv_b_pw2[0]))
    rwkv_p = dict(gn=row(norm_mix[1]), xmix=rwkv_x_mix[0], wr=bf(rwkv_w_r[0]), wk=bf(rwkv_w_k[0]),
                  wv=bf(rwkv_w_v[0]), w0=row(rwkv_w0[0]), w1=bf(rwkv_w1[0]), w2=bf(rwkv_w2[0]),
                  a0=row(rwkv_a0[0]), a1=bf(rwkv_a1[0]), a2=bf(rwkv_a2[0]), g1=bf(rwkv_g1[0]),
                  g2=bf(rwkv_g2[0]), kk=row(rwkv_k_k[0]), ka=row(rwkv_k_a[0]))
    post_p = (row(rwkv_gn_g[0]), row(rwkv_gn_b[0]), row(rwkv_r_k[0]), bf(rwkv_w_o[0]))
    mlp_p = [(row(norm_mlp[i]), bf(w_mlp_in[i]), bf(w_mlp_out[i])) for i in range(depth)]
    gf = row(norm_final)

    def mlp(x2d, i, final):
        return _mlp(x2d, *mlp_p[i], gf, final=final, tm=_row_tile(x2d.shape[0], 256))

    def seq_trunk(x, buf0, sh0, s0, need_out):
        b, t, _ = x.shape
        h, nb = _conv_seq(x, buf0, *conv_p, tm=_row_tile(t, 256))
        h = mlp(h.reshape(b * t, D), 0, False).reshape(b, t, D)
        r, lw, k, v, a, bb, g, sh = _rwkv_pre_seq(h, sh0, rwkv_p, tm=_row_tile(t, 256))
        y, s1 = _wkv_seq(r, lw, k, v, a, bb, s0, L=_row_tile(t, 64))
        if not need_out:
            return None, nb, sh, s1
        f2 = lambda z: z.reshape(b * t, D)
        h = _rwkv_post(f2(y), f2(r), f2(k), f2(v), f2(g), f2(h), *post_p, tm=_row_tile(b * t, 256))
        return mlp(h, 1, True).reshape(b, t, D), nb, sh, s1

    zero_buf = jnp.zeros((1, BUF_PAD, D), F32)
    zero_sh = jnp.zeros((1, 1, D), F32)
    zero_s = jnp.zeros((1, nh, HEAD, HEAD), F32)
    _, m_buf, m_sh, m_s = seq_trunk(meta_tokens.astype(F32)[None], zero_buf, zero_sh, zero_s, False)

    y_prompt, p_buf, p_sh, p_s = seq_trunk(x_prompt, m_buf, m_sh, m_s, True)
    conv_prompt = p_buf[:, BUF_PAD - CONV_BUF:][None]
    shift_prompt = p_sh.reshape(1, B, D)
    wkv_prompt = p_s[None].astype(state_wkv.dtype)

    xs = x_sample.reshape(SB, D)
    buft = jnp.swapaxes(state_conv[0], 0, 1)
    h, u = _conv_step(xs, buft, *conv_p)
    conv_sample = jnp.concatenate([state_conv[0][:, 1:], u[:, None, :]], axis=1)[None]
    h = mlp(h, 0, False)
    r, lw, k, v, a, bb, g, sh = _rwkv_pre_step(h, state_shift[0], rwkv_p)
    y, s_new = _wkv_step(r, lw, k, v, a, bb, state_wkv[0].astype(F32), bt=8)
    h = _rwkv_post(y, r, k, v, g, h, *post_p, tm=SB)
    y_sample = mlp(h, 1, True).reshape(SB, 1, D)
    shift_sample = sh[None]
    wkv_sample = s_new[None].astype(state_wkv.dtype)

    return (y_prompt, y_sample, conv_prompt, shift_prompt, wkv_prompt,
            conv_sample, shift_sample, wkv_sample)
```

```python
import functools
import math

import jax
import jax.numpy as jnp
from jax import lax
from jax.experimental import pallas as pl
from jax.experimental.pallas import tpu as pltpu

F32 = jnp.float32
BF16 = jnp.bfloat16

RMS_EPS = 1e-6
LN_EPS = 1e-5
GN_EPS = 64e-5
L2_EPS = 1e-12
HEAD = 64
CONV_W = 31
CONV_BUF = CONV_W - 1
BUF_PAD = 32
MXU_TILE = 256
LANE = 128
SUBLANE = 8
VMEM_LIMIT = 52 * 1024 * 1024


def _mm(a, w):
    return jnp.dot(a.astype(BF16), w, preferred_element_type=F32)


def _dot_nt(a, b):
    return lax.dot_general(a, b, (((1,), (1,)), ((), ())), preferred_element_type=F32)


def _dot_tn(a, b):
    return lax.dot_general(a, b, (((0,), (0,)), ((), ())), preferred_element_type=F32)


def _rmsnorm(x, g):
    ms = jnp.mean(x * x, axis=-1, keepdims=True)
    return x * lax.rsqrt(ms + RMS_EPS) * g


def _head_ones():
    r = lax.broadcasted_iota(jnp.int32, (MXU_TILE, MXU_TILE), 0) // HEAD
    c = lax.broadcasted_iota(jnp.int32, (MXU_TILE, MXU_TILE), 1) // HEAD
    return jnp.where(r == c, 1.0, 0.0).astype(BF16)


def _seg_sum(x, ones_bd):
    hi = x.astype(BF16)
    lo = (x - hi.astype(F32)).astype(BF16)
    outs = []
    for c in range(x.shape[-1] // MXU_TILE):
        sl = slice(c * MXU_TILE, (c + 1) * MXU_TILE)
        s = jnp.dot(hi[:, sl], ones_bd, preferred_element_type=F32)
        s = s + jnp.dot(lo[:, sl], ones_bd, preferred_element_type=F32)
        outs.append(s)
    return jnp.concatenate(outs, axis=-1)


def _const_spec(shape):
    nd = len(shape)
    return pl.BlockSpec(shape, lambda *_: (0,) * nd)


def _params(sem):
    return pltpu.CompilerParams(dimension_semantics=sem, vmem_limit_bytes=VMEM_LIMIT)


def _conv_pre(x, g, w1, b1):
    d = x.shape[-1]
    u = _mm(_rmsnorm(x, g), w1) + b1
    return u[:, :d] * jax.nn.sigmoid(u[:, d:])


def _conv_post(x, c, lng, lnb, w2, b2):
    mu = jnp.mean(c, axis=-1, keepdims=True)
    cc = c - mu
    var = jnp.mean(cc * cc, axis=-1, keepdims=True)
    cn = cc * lax.rsqrt(var + LN_EPS) * lng + lnb
    return x + _mm(cn * jax.nn.sigmoid(cn), w2) + b2


def _conv_seq_kernel(x_ref, buf0_ref, g_ref, w1_ref, b1_ref, wdw_ref, bdw_ref, lng_ref, lnb_ref,
                     w2_ref, b2_ref, h_ref, nb_ref, ubuf, cbuf, shl, *, tm):
    t = pl.program_id(1)
    d = x_ref.shape[-1]

    @pl.when(t == 0)
    def _():
        ubuf[0:BUF_PAD, :] = buf0_ref[0]

    x = x_ref[0]
    ubuf[BUF_PAD:BUF_PAD + tm, :] = _conv_pre(x, g_ref[...], w1_ref[...], b1_ref[...])
    off = BUF_PAD - CONV_BUF
    span = tm + BUF_PAD - SUBLANE
    for l in range(d // LANE):
        ls = slice(l * LANE, (l + 1) * LANE)
        for p in range(1, SUBLANE):
            shl[p - 1] = ubuf[p:p + span, ls]
        acc = jnp.broadcast_to(bdw_ref[:, ls], (tm, LANE))
        for j in range(CONV_W):
            q, p = divmod(off + j, SUBLANE)
            src = ubuf[SUBLANE * q:SUBLANE * q + tm, ls] if p == 0 else shl[p - 1, SUBLANE * q:SUBLANE * q + tm, :]
            acc = acc + wdw_ref[j:j + 1, ls] * src
        cbuf[:, ls] = acc
    h_ref[0] = _conv_post(x, cbuf[...], lng_ref[...], lnb_ref[...], w2_ref[...], b2_ref[...])
    tail = ubuf[tm:tm + BUF_PAD, :]
    ubuf[0:BUF_PAD, :] = tail

    @pl.when(t == pl.num_programs(1) - 1)
    def _():
        nb_ref[0] = tail


def _conv_seq(x, buf0, g, w1, b1, wdw, bdw, lng, lnb, w2, b2, *, tm):
    B, T, D = x.shape
    nt = T // tm
    shared = buf0.shape[0] == 1
    buf_map = (lambda b, t: (0, 0, 0)) if shared else (lambda b, t: (b, 0, 0))
    return pl.pallas_call(
        functools.partial(_conv_seq_kernel, tm=tm),
        grid=(B, nt),
        in_specs=[
            pl.BlockSpec((1, tm, D), lambda b, t: (b, t, 0)),
            pl.BlockSpec((1, BUF_PAD, D), buf_map),
            _const_spec((1, D)), _const_spec((D, 2 * D)), _const_spec((1, 2 * D)),
            _const_spec((CONV_W, D)), _const_spec((1, D)), _const_spec((1, D)), _const_spec((1, D)),
            _const_spec((D, D)), _const_spec((1, D)),
        ],
        out_specs=[
            pl.BlockSpec((1, tm, D), lambda b, t: (b, t, 0)),
            pl.BlockSpec((1, BUF_PAD, D), lambda b, t: (b, 0, 0)),
        ],
        out_shape=[jax.ShapeDtypeStruct((B, T, D), F32), jax.ShapeDtypeStruct((B, BUF_PAD, D), F32)],
        scratch_shapes=[pltpu.VMEM((BUF_PAD + tm, D), F32), pltpu.VMEM((tm, D), F32),
                        pltpu.VMEM((SUBLANE - 1, tm + BUF_PAD - SUBLANE, LANE), F32)],
        compiler_params=_params(("arbitrary", "arbitrary")),
        name="conv_seq",
    )(x, buf0, g, w1, b1, wdw, bdw, lng, lnb, w2, b2)


def _conv_step_kernel(x_ref, buf_ref, g_ref, w1_ref, b1_ref, wdw_ref, bdw_ref, lng_ref, lnb_ref,
                      w2_ref, b2_ref, h_ref, nb_ref, ubuf, cbuf, *, rb):
    i = pl.program_id(0)

    @pl.when(i == 0)
    def _():
        ubuf[...] = _conv_pre(x_ref[...], g_ref[...], w1_ref[...], b1_ref[...])

    rows = pl.ds(pl.multiple_of(i * rb, rb), rb)
    u = ubuf[rows, :]
    w_hist = wdw_ref[0:CONV_BUF, :]
    w_new = wdw_ref[CONV_BUF:CONV_W, :]
    crow = []
    for j in range(rb):
        hist = buf_ref[j]
        uj = u[j:j + 1, :]
        crow.append(bdw_ref[...] + jnp.sum(hist * w_hist, axis=0, keepdims=True) + w_new * uj)
        nb_ref[j, 0:CONV_BUF - 1, :] = buf_ref[j, 1:CONV_BUF, :]
        nb_ref[j, CONV_BUF - 1:CONV_BUF, :] = uj
    cbuf[rows, :] = jnp.concatenate(crow, axis=0)

    @pl.when(i == pl.num_programs(0) - 1)
    def _():
        h_ref[...] = _conv_post(x_ref[...], cbuf[...], lng_ref[...], lnb_ref[...], w2_ref[...], b2_ref[...])


def _conv_step(x, buf, g, w1, b1, wdw, bdw, lng, lnb, w2, b2, *, rb):
    R, D = x.shape
    hist = pl.BlockSpec((rb, CONV_BUF, D), lambda i: (i, 0, 0))
    return pl.pallas_call(
        functools.partial(_conv_step_kernel, rb=rb),
        grid=(R // rb,),
        in_specs=[
            _const_spec((R, D)), hist,
            _const_spec((1, D)), _const_spec((D, 2 * D)), _const_spec((1, 2 * D)),
            _const_spec((CONV_W, D)), _const_spec((1, D)), _const_spec((1, D)), _const_spec((1, D)),
            _const_spec((D, D)), _const_spec((1, D)),
        ],
        out_specs=[_const_spec((R, D)), hist],
        out_shape=[jax.ShapeDtypeStruct((R, D), F32), jax.ShapeDtypeStruct((R, CONV_BUF, D), F32)],
        scratch_shapes=[pltpu.VMEM((R, D), F32), pltpu.VMEM((R, D), F32)],
        compiler_params=_params(("arbitrary",)),
        name="conv_step",
    )(x, buf, g, w1, b1, wdw, bdw, lng, lnb, w2, b2)


def _mlp_kernel(x_ref, g_ref, win_ref, wout_ref, gf_ref, o_ref, *, final):
    x = x_ref[...]
    hid = jnp.maximum(_mm(_rmsnorm(x, g_ref[...]), win_ref[...]), 0.0)
    y = x + _mm(hid * hid, wout_ref[...])
    if final:
        y = _rmsnorm(y, gf_ref[...])
    o_ref[...] = y


def _mlp(x, g, win, wout, gf, *, final, tm):
    R, D = x.shape
    F = win.shape[1]
    return pl.pallas_call(
        functools.partial(_mlp_kernel, final=final),
        grid=(R // tm,),
        in_specs=[
            pl.BlockSpec((tm, D), lambda i: (i, 0)),
            _const_spec((1, D)), _const_spec((D, F)), _const_spec((F, D)), _const_spec((1, D)),
        ],
        out_specs=pl.BlockSpec((tm, D), lambda i: (i, 0)),
        out_shape=jax.ShapeDtypeStruct((R, D), F32),
        compiler_params=_params(("arbitrary",)),
        name="mlp",
    )(x, g, win, wout, gf)


def _rwkv_pre_math(hn, prev, xmix, wr, wk, wv, w0, w1, w2, a0, a1, a2, g1, g2, kk_w, ka_w):
    xx = prev - hn
    xr = hn + xx * xmix[0:1]
    xw = hn + xx * xmix[1:2]
    xk = hn + xx * xmix[2:3]
    xv = hn + xx * xmix[3:4]
    xa = hn + xx * xmix[4:5]
    xg = hn + xx * xmix[5:6]
    r = _mm(xr, wr)
    k = _mm(xk, wk)
    v = _mm(xv, wv)
    z = w0 + _mm(jnp.tanh(_mm(xw, w1)), w2)
    w_log = -(jnp.maximum(-z, 0.0) + jnp.log(1.0 + jnp.exp(-jnp.abs(z)))) - 0.5
    lw = -jnp.exp(w_log)
    iclr = jax.nn.sigmoid(a0 + _mm(_mm(xa, a1), a2))
    gate = _mm(jax.nn.sigmoid(_mm(xg, g1)), g2)
    kk = k * kk_w
    nrm = jnp.sqrt(_seg_sum(kk * kk, _head_ones()))
    kk = kk / jnp.maximum(nrm, L2_EPS)
    k2 = k * (1.0 + (iclr - 1.0) * ka_w)
    return r, lw, k2, v, -kk, kk * iclr, gate


def _rwkv_weight_specs(D, p):
    return [
        _const_spec((1, D)), _const_spec((6, D)),
        _const_spec((D, D)), _const_spec((D, D)), _const_spec((D, D)),
        _const_spec((1, D)), _const_spec(p["w1"].shape), _const_spec(p["w2"].shape),
        _const_spec((1, D)), _const_spec(p["a1"].shape), _const_spec(p["a2"].shape),
        _const_spec(p["g1"].shape), _const_spec(p["g2"].shape),
        _const_spec((1, D)), _const_spec((1, D)),
    ]


def _rwkv_weight_args(p):
    return (p["gn"], p["xmix"], p["wr"], p["wk"], p["wv"], p["w0"], p["w1"], p["w2"],
            p["a0"], p["a1"], p["a2"], p["g1"], p["g2"], p["kk"], p["ka"])


def _rwkv_pre_step_kernel(h_ref, prev_ref, gn_ref, xmix_ref, wr_ref, wk_ref, wv_ref, w0_ref, w1_ref, w2_ref,
                          a0_ref, a1_ref, a2_ref, g1_ref, g2_ref, kk_ref, ka_ref,
                          r_ref, lw_ref, k_ref, v_ref, a_ref, b_ref, g_ref, sh_ref):
    hn = _rmsnorm(h_ref[...], gn_ref[...])
    outs = _rwkv_pre_math(hn, prev_ref[...], xmix_ref[...], wr_ref[...], wk_ref[...], wv_ref[...], w0_ref[...],
                          w1_ref[...], w2_ref[...], a0_ref[...], a1_ref[...], a2_ref[...], g1_ref[...],
                          g2_ref[...], kk_ref[...], ka_ref[...])
    for o_ref, o in zip((r_ref, lw_ref, k_ref, v_ref, a_ref, b_ref, g_ref), outs):
        o_ref[...] = o
    sh_ref[...] = hn


def _rwkv_pre_step(h, prev, p):
    R, D = h.shape
    full = _const_spec((R, D))
    return pl.pallas_call(
        _rwkv_pre_step_kernel,
        grid=(1,),
        in_specs=[full, full] + _rwkv_weight_specs(D, p),
        out_specs=[full] * 8,
        out_shape=[jax.ShapeDtypeStruct((R, D), F32)] * 8,
        compiler_params=_params(("arbitrary",)),
        name="rwkv_pre_step",
    )(h, prev, *_rwkv_weight_args(p))


def _rwkv_post_math(y, r, k, v, g, h, gng, gnb, rk, wo):
    ones_bd = _head_ones()
    inv_n = 1.0 / HEAD
    mu = _seg_sum(y, ones_bd) * inv_n
    yc = y - mu
    var = _seg_sum(yc * yc, ones_bd) * inv_n
    yn = yc * lax.rsqrt(var + GN_EPS) * gng + gnb
    bonus = _seg_sum(r * k * rk, ones_bd) * v
    return h + _mm((yn + bonus) * g, wo)


def _wkv_chunk(r, lw, k, v, a, b, state, *, L):
    nh = state.shape[0]
    row = lax.broadcasted_iota(jnp.int32, lw.shape, 0)
    cs = lw
    s = 1
    while s < L:
        cs = cs + jnp.where(row >= s, pltpu.roll(cs, s, 0), 0.0)
        s *= 2
    gam = jnp.exp(cs)
    ginv = jnp.exp(-cs)
    g_last = gam[L - 1:L, :]
    rt_f = r * gam
    bt_f = b * ginv
    kt_f = k * ginv
    rt = rt_f.astype(BF16)
    at = (a * jnp.exp(cs - lw)).astype(BF16)
    bt = bt_f.astype(BF16)
    kt = kt_f.astype(BF16)
    bg = (bt_f * g_last).astype(BF16)
    kg = (kt_f * g_last).astype(BF16)
    vb = v.astype(BF16)

    ri = lax.broadcasted_iota(jnp.int32, (L, 2 * L), 0)
    ci = lax.broadcasted_iota(jnp.int32, (L, 2 * L), 1)
    left = ci < L
    cm = jnp.where(left, ci, ci - L)
    strict2 = ri > cm
    incl2 = ri >= cm
    eye_right = jnp.where(jnp.logical_and(ri == cm, jnp.logical_not(left)), 1.0, 0.0)
    zeros_l = jnp.zeros((L, HEAD), BF16)
    heads = range(nh)
    hs = lambda x, h: x[:, h * HEAD:(h + 1) * HEAD]
    dot = lambda x, y: jnp.dot(x, y, preferred_element_type=F32)
    cat0 = lambda x, y: jnp.concatenate([x, y], axis=0)
    cat1 = lambda x, y: jnp.concatenate([x, y], axis=1)

    P = [_dot_nt(cat0(hs(at, h), hs(rt, h)), cat0(hs(bt, h), hs(kt, h))) for h in heads]
    m_a = [jnp.where(strict2, p[:L], 0.0) for p in P]
    m_r = [jnp.where(incl2, p[L:], 0.0).astype(BF16) for p in P]
    makv = [dot(m_a[h][:, L:].astype(BF16), hs(vb, h)).astype(BF16) for h in heads]
    W = [jnp.where(left, m, 0.0) + eye_right for m in m_a]
    for _ in range(int(math.log2(L))):
        Z = [dot(W[h][:, :L].astype(BF16), W[h].astype(BF16)) for h in heads]
        W = [jnp.where(left, Z[h], W[h] + Z[h]) for h in heads]
    tfin = [w[:, L:].astype(BF16) for w in W]
    TX = [dot(tfin[h], cat1(hs(at, h), makv[h])).astype(BF16) for h in heads]
    RY = [dot(m_r[h], cat0(TX[h], cat1(zeros_l, hs(vb, h)))) for h in heads]
    rbar = [(hs(rt_f, h) + RY[h][:, :HEAD]).astype(BF16) for h in heads]
    GH = [_dot_tn(TX[h], hs(bg, h)) for h in heads]
    VK = [_dot_tn(hs(vb, h), hs(kg, h)) for h in heads]
    ys = []
    for h in heads:
        S = state[h]
        Sb = S.astype(BF16)
        ys.append(_dot_nt(rbar[h], Sb) + RY[h][:, HEAD:])
        state[h] = S * hs(g_last, h) + dot(Sb, GH[h][:HEAD].astype(BF16)) + GH[h][HEAD:] + VK[h]
    return ys


def _rwkv_seq_kernel(h_ref, sh0_ref, s0_ref, gn_ref, xmix_ref, wr_ref, wk_ref, wv_ref, w0_ref, w1_ref, w2_ref,
                     a0_ref, a1_ref, a2_ref, g1_ref, g2_ref, kk_ref, ka_ref, gng_ref, gnb_ref, rk_ref, wo_ref,
                     o_ref, sh_ref, sout_ref, carry, state, rb, lwb, kb, vb, ab, bb, gb, yb, *, tm, L):
    t = pl.program_id(1)

    @pl.when(t == 0)
    def _():
        carry[...] = sh0_ref[0]
        state[...] = s0_ref[0]

    h = h_ref[0]
    hn = _rmsnorm(h, gn_ref[...])
    row = lax.broadcasted_iota(jnp.int32, hn.shape, 0)
    prev = jnp.where(row == 0, carry[...], pltpu.roll(hn, 1, 0))
    last = hn[tm - 1:tm, :]
    carry[...] = last
    outs = _rwkv_pre_math(hn, prev, xmix_ref[...], wr_ref[...], wk_ref[...], wv_ref[...], w0_ref[...],
                          w1_ref[...], w2_ref[...], a0_ref[...], a1_ref[...], a2_ref[...], g1_ref[...],
                          g2_ref[...], kk_ref[...], ka_ref[...])
    for buf, o in zip((rb, lwb, kb, vb, ab, bb, gb), outs):
        buf[...] = o

    def chunk(c):
        rows = pl.ds(pl.multiple_of(c * L, L), L)
        ys = _wkv_chunk(rb[rows, :], lwb[rows, :], kb[rows, :], vb[rows, :], ab[rows, :], bb[rows, :], state, L=L)
        for i, y in enumerate(ys):
            yb[rows, i * HEAD:(i + 1) * HEAD] = y

    if tm == L:
        chunk(0)
    else:
        def body(c, carry_):
            chunk(c)
            return carry_
        lax.fori_loop(0, tm // L, body, 0)

    o_ref[0] = _rwkv_post_math(yb[...], rb[...], kb[...], vb[...], gb[...], h, gng_ref[...], gnb_ref[...],
                               rk_ref[...], wo_ref[...])

    @pl.when(t == pl.num_programs(1) - 1)
    def _():
        sh_ref[0] = last
        sout_ref[0] = state[...]


def _rwkv_seq(h, sh0, s0, p, post_p, *, tm, L):
    B, T, D = h.shape
    nh = D // HEAD
    shared = sh0.shape[0] == 1
    sh_map = (lambda b, t: (0, 0, 0)) if shared else (lambda b, t: (b, 0, 0))
    s_map = (lambda b, t: (0, 0, 0, 0)) if shared else (lambda b, t: (b, 0, 0, 0))
    tile = pl.BlockSpec((1, tm, D), lambda b, t: (b, t, 0))
    return pl.pallas_call(
        functools.partial(_rwkv_seq_kernel, tm=tm, L=L),
        grid=(B, T // tm),
        in_specs=([tile, pl.BlockSpec((1, 1, D), sh_map), pl.BlockSpec((1, nh, HEAD, HEAD), s_map)]
                  + _rwkv_weight_specs(D, p) + [_const_spec((1, D))] * 3 + [_const_spec((D, D))]),
        out_specs=[tile, pl.BlockSpec((1, 1, D), lambda b, t: (b, 0, 0)),
                   pl.BlockSpec((1, nh, HEAD, HEAD), lambda b, t: (b, 0, 0, 0))],
        out_shape=[jax.ShapeDtypeStruct((B, T, D), F32), jax.ShapeDtypeStruct((B, 1, D), F32),
                   jax.ShapeDtypeStruct((B, nh, HEAD, HEAD), F32)],
        scratch_shapes=[pltpu.VMEM((1, D), F32), pltpu.VMEM((nh, HEAD, HEAD), F32)]
                       + [pltpu.VMEM((tm, D), F32)] * 8,
        compiler_params=_params(("arbitrary", "arbitrary")),
        name="rwkv_seq",
    )(h, sh0, s0, *_rwkv_weight_args(p), *post_p)


def _wkv_step_kernel(r_ref, lw_ref, k_ref, v_ref, a_ref, b_ref, s_ref, y_ref, sout_ref):
    S = s_ref[...]
    sa = jnp.sum(S * a_ref[...], axis=-1, keepdims=True)
    S = S * jnp.exp(lw_ref[...]) + sa * b_ref[...] + v_ref[...] * k_ref[...]
    sout_ref[...] = S
    y_ref[...] = jnp.sum(S * r_ref[...], axis=-1, keepdims=True)


def _wkv_step(r, lw, k, v, a, b, s, *, bt):
    R, D = r.shape
    nh = D // HEAD
    rowv = lambda x: x.reshape(R, nh, 1, HEAD)
    row_spec = pl.BlockSpec((bt, nh, 1, HEAD), lambda i: (i, 0, 0, 0))
    col_spec = pl.BlockSpec((bt, nh, HEAD, 1), lambda i: (i, 0, 0, 0))
    st_spec = pl.BlockSpec((bt, nh, HEAD, HEAD), lambda i: (i, 0, 0, 0))
    y, s_new = pl.pallas_call(
        _wkv_step_kernel,
        grid=(R // bt,),
        in_specs=[row_spec, row_spec, row_spec, col_spec, row_spec, row_spec, st_spec],
        out_specs=[col_spec, st_spec],
        out_shape=[jax.ShapeDtypeStruct((R, nh, HEAD, 1), F32), jax.ShapeDtypeStruct((R, nh, HEAD, HEAD), F32)],
        compiler_params=_params(("arbitrary",)),
        name="wkv_step",
    )(rowv(r), rowv(lw), rowv(k), v.reshape(R, nh, HEAD, 1), rowv(a), rowv(b), s)
    return y.reshape(R, D), s_new


def _rwkv_post_kernel(y_ref, r_ref, k_ref, v_ref, g_ref, h_ref, gng_ref, gnb_ref, rk_ref, wo_ref, o_ref):
    o_ref[...] = _rwkv_post_math(y_ref[...], r_ref[...], k_ref[...], v_ref[...], g_ref[...], h_ref[...],
                                 gng_ref[...], gnb_ref[...], rk_ref[...], wo_ref[...])


def _rwkv_post(y, r, k, v, g, h, gng, gnb, rk, wo, *, tm):
    R, D = y.shape
    tile = pl.BlockSpec((tm, D), lambda i: (i, 0))
    return pl.pallas_call(
        _rwkv_post_kernel,
        grid=(R // tm,),
        in_specs=[tile] * 6 + [_const_spec((1, D))] * 3 + [_const_spec((D, D))],
        out_specs=tile,
        out_shape=jax.ShapeDtypeStruct((R, D), F32),
        compiler_params=_params(("arbitrary",)),
        name="rwkv_post",
    )(y, r, k, v, g, h, gng, gnb, rk, wo)


def _row_tile(rows, target):
    tm = min(rows, target)
    while rows % tm:
        tm //= 2
    return tm


def kernel(x_prompt, x_sample, state_conv, state_shift, state_wkv, meta_tokens, norm_mix, norm_mlp, norm_final, conv_w_pw1, conv_b_pw1, conv_w_dw, conv_b_dw, conv_ln_g, conv_ln_b, conv_w_pw2, conv_b_pw2, rwkv_x_mix, rwkv_w_r, rwkv_w_k, rwkv_w_v, rwkv_w_o, rwkv_w0, rwkv_w1, rwkv_w2, rwkv_a0, rwkv_a1, rwkv_a2, rwkv_g1, rwkv_g2, rwkv_k_k, rwkv_k_a, rwkv_r_k, rwkv_gn_g, rwkv_gn_b, w_mlp_in, w_mlp_out):
    B, T, D = x_prompt.shape
    SB = x_sample.shape[0]
    nh = D // HEAD
    depth = norm_mix.shape[0]
    assert depth == 2 and x_sample.shape[1] == 1 and D % MXU_TILE == 0

    row = lambda x: x.reshape(1, -1).astype(F32)
    bf = lambda x: x.astype(BF16)

    conv_p = (row(norm_mix[0]), bf(conv_w_pw1[0]), row(conv_b_pw1[0]), conv_w_dw[0], row(conv_b_dw[0]),
              row(conv_ln_g[0]), row(conv_ln_b[0]), bf(conv_w_pw2[0]), row(conv_b_pw2[0]))
    rwkv_p = dict(gn=row(norm_mix[1]), xmix=rwkv_x_mix[0], wr=bf(rwkv_w_r[0]), wk=bf(rwkv_w_k[0]),
                  wv=bf(rwkv_w_v[0]), w0=row(rwkv_w0[0]), w1=bf(rwkv_w1[0]), w2=bf(rwkv_w2[0]),
                  a0=row(rwkv_a0[0]), a1=bf(rwkv_a1[0]), a2=bf(rwkv_a2[0]), g1=bf(rwkv_g1[0]),
                  g2=bf(rwkv_g2[0]), kk=row(rwkv_k_k[0]), ka=row(rwkv_k_a[0]))
    post_p = (row(rwkv_gn_g[0]), row(rwkv_gn_b[0]), row(rwkv_r_k[0]), bf(rwkv_w_o[0]))
    mlp_p = [(row(norm_mlp[i]), bf(w_mlp_in[i]), bf(w_mlp_out[i])) for i in range(depth)]
    gf = row(norm_final)

    def mlp(x2d, i, final):
        return _mlp(x2d, *mlp_p[i], gf, final=final, tm=_row_tile(x2d.shape[0], 256))

    def seq_trunk(x, buf0, sh0, s0, need_out):
        b, t, _ = x.shape
        h, nb = _conv_seq(x, buf0, *conv_p, tm=_row_tile(t, 256))
        h = mlp(h.reshape(b * t, D), 0, False).reshape(b, t, D)
        h, sh, s1 = _rwkv_seq(h, sh0, s0, rwkv_p, post_p, tm=_row_tile(t, 256), L=_row_tile(t, 64))
        if not need_out:
            return None, nb, sh, s1
        return mlp(h.reshape(b * t, D), 1, True).reshape(b, t, D), nb, sh, s1

    zero_buf = jnp.zeros((1, BUF_PAD, D), F32)
    zero_sh = jnp.zeros((1, 1, D), F32)
    zero_s = jnp.zeros((1, nh, HEAD, HEAD), F32)
    _, m_buf, m_sh, m_s = seq_trunk(meta_tokens.astype(F32)[None], zero_buf, zero_sh, zero_s, False)

    y_prompt, p_buf, p_sh, p_s = seq_trunk(x_prompt, m_buf, m_sh, m_s, True)
    conv_prompt = p_buf[:, BUF_PAD - CONV_BUF:][None]
    shift_prompt = p_sh.reshape(1, B, D)
    wkv_prompt = p_s[None].astype(state_wkv.dtype)

    xs = x_sample.reshape(SB, D)
    h, conv_sample = _conv_step(xs, state_conv[0], *conv_p, rb=8)
    conv_sample = conv_sample[None]
    h = mlp(h, 0, False)
    r, lw, k, v, a, bb, g, sh = _rwkv_pre_step(h, state_shift[0], rwkv_p)
    y, s_new = _wkv_step(r, lw, k, v, a, bb, state_wkv[0].astype(F32), bt=8)
    h = _rwkv_post(y, r, k, v, g, h, *post_p, tm=SB)
    y_sample = mlp(h, 1, True).reshape(SB, 1, D)
    shift_sample = sh[None]
    wkv_sample = s_new[None].astype(state_wkv.dtype)

    return (y_prompt, y_sample, conv_prompt, shift_prompt, wkv_prompt,
            conv_sample, shift_sample, wkv_sample)
```

```python
import functools
import math

import jax
import jax.numpy as jnp
from jax import lax
from jax.experimental import pallas as pl
from jax.experimental.pallas import tpu as pltpu

F32 = jnp.float32
BF16 = jnp.bfloat16

RMS_EPS = 1e-6
LN_EPS = 1e-5
GN_EPS = 64e-5
L2_EPS = 1e-12
HEAD = 64
CONV_W = 31
CONV_BUF = CONV_W - 1
BUF_PAD = 32
MXU_TILE = 256
LANE = 128
SUBLANE = 8
VMEM_LIMIT = 52 * 1024 * 1024


def _mm(a, w):
    return jnp.dot(a.astype(BF16), w, preferred_element_type=F32)


def _dot_nt(a, b):
    return lax.dot_general(a, b, (((1,), (1,)), ((), ())), preferred_element_type=F32)


def _dot_tn(a, b):
    return lax.dot_general(a, b, (((0,), (0,)), ((), ())), preferred_element_type=F32)


def _rmsnorm(x, g):
    ms = jnp.mean(x * x, axis=-1, keepdims=True)
    return x * lax.rsqrt(ms + RMS_EPS) * g


def _head_ones():
    r = lax.broadcasted_iota(jnp.int32, (MXU_TILE, MXU_TILE), 0) // HEAD
    c = lax.broadcasted_iota(jnp.int32, (MXU_TILE, MXU_TILE), 1) // HEAD
    return jnp.where(r == c, 1.0, 0.0).astype(BF16)


def _seg_sum(x, ones_bd):
    hi = x.astype(BF16)
    lo = (x - hi.astype(F32)).astype(BF16)
    outs = []
    for c in range(x.shape[-1] // MXU_TILE):
        sl = slice(c * MXU_TILE, (c + 1) * MXU_TILE)
        s = jnp.dot(hi[:, sl], ones_bd, preferred_element_type=F32)
        s = s + jnp.dot(lo[:, sl], ones_bd, preferred_element_type=F32)
        outs.append(s)
    return jnp.concatenate(outs, axis=-1)


def _const_spec(shape):
    nd = len(shape)
    return pl.BlockSpec(shape, lambda *_: (0,) * nd)


def _params(sem):
    return pltpu.CompilerParams(dimension_semantics=sem, vmem_limit_bytes=VMEM_LIMIT)


def _conv_pre(x, g, w1, b1):
    d = x.shape[-1]
    u = _mm(_rmsnorm(x, g), w1) + b1
    return u[:, :d] * jax.nn.sigmoid(u[:, d:])


def _conv_post(x, c, lng, lnb, w2, b2):
    mu = jnp.mean(c, axis=-1, keepdims=True)
    cc = c - mu
    var = jnp.mean(cc * cc, axis=-1, keepdims=True)
    cn = cc * lax.rsqrt(var + LN_EPS) * lng + lnb
    return x + _mm(cn * jax.nn.sigmoid(cn), w2) + b2


def _conv_seq_kernel(x_ref, buf0_ref, g_ref, w1_ref, b1_ref, wdw_ref, bdw_ref, lng_ref, lnb_ref,
                     w2_ref, b2_ref, h_ref, nb_ref, ubuf, cbuf, shl, *, tm):
    t = pl.program_id(1)
    d = x_ref.shape[-1]

    @pl.when(t == 0)
    def _():
        ubuf[0:BUF_PAD, :] = buf0_ref[0]

    x = x_ref[0]
    ubuf[BUF_PAD:BUF_PAD + tm, :] = _conv_pre(x, g_ref[...], w1_ref[...], b1_ref[...])
    off = BUF_PAD - CONV_BUF
    span = tm + BUF_PAD - SUBLANE
    for l in range(d // LANE):
        ls = slice(l * LANE, (l + 1) * LANE)
        for p in range(1, SUBLANE):
            shl[p - 1] = ubuf[p:p + span, ls]
        acc = jnp.broadcast_to(bdw_ref[:, ls], (tm, LANE))
        for j in range(CONV_W):
            q, p = divmod(off + j, SUBLANE)
            src = ubuf[SUBLANE * q:SUBLANE * q + tm, ls] if p == 0 else shl[p - 1, SUBLANE * q:SUBLANE * q + tm, :]
            acc = acc + wdw_ref[j:j + 1, ls] * src
        cbuf[:, ls] = acc
    h_ref[0] = _conv_post(x, cbuf[...], lng_ref[...], lnb_ref[...], w2_ref[...], b2_ref[...])
    tail = ubuf[tm:tm + BUF_PAD, :]
    ubuf[0:BUF_PAD, :] = tail

    @pl.when(t == pl.num_programs(1) - 1)
    def _():
        nb_ref[0] = tail


def _conv_seq(x, buf0, g, w1, b1, wdw, bdw, lng, lnb, w2, b2, *, tm):
    B, T, D = x.shape
    nt = T // tm
    shared = buf0.shape[0] == 1
    buf_map = (lambda b, t: (0, 0, 0)) if shared else (lambda b, t: (b, 0, 0))
    return pl.pallas_call(
        functools.partial(_conv_seq_kernel, tm=tm),
        grid=(B, nt),
        in_specs=[
            pl.BlockSpec((1, tm, D), lambda b, t: (b, t, 0)),
            pl.BlockSpec((1, BUF_PAD, D), buf_map),
            _const_spec((1, D)), _const_spec((D, 2 * D)), _const_spec((1, 2 * D)),
            _const_spec((CONV_W, D)), _const_spec((1, D)), _const_spec((1, D)), _const_spec((1, D)),
            _const_spec((D, D)), _const_spec((1, D)),
        ],
        out_specs=[
            pl.BlockSpec((1, tm, D), lambda b, t: (b, t, 0)),
            pl.BlockSpec((1, BUF_PAD, D), lambda b, t: (b, 0, 0)),
        ],
        out_shape=[jax.ShapeDtypeStruct((B, T, D), F32), jax.ShapeDtypeStruct((B, BUF_PAD, D), F32)],
        scratch_shapes=[pltpu.VMEM((BUF_PAD + tm, D), F32), pltpu.VMEM((tm, D), F32),
                        pltpu.VMEM((SUBLANE - 1, tm + BUF_PAD - SUBLANE, LANE), F32)],
        compiler_params=_params(("arbitrary", "arbitrary")),
        name="conv_seq",
    )(x, buf0, g, w1, b1, wdw, bdw, lng, lnb, w2, b2)


def _conv_step_kernel(x_ref, buf_ref, wj_ref, wlast_ref, g_ref, w1_ref, b1_ref, bdw_ref, lng_ref, lnb_ref,
                      w2_ref, b2_ref, h_ref, nb_ref, ubuf, cbuf):
    j = pl.program_id(0)
    n_hist = pl.num_programs(0) - 1

    @pl.when(j == 0)
    def _():
        u = _conv_pre(x_ref[...], g_ref[...], w1_ref[...], b1_ref[...])
        ubuf[...] = u
        cbuf[...] = bdw_ref[...] + wlast_ref[...] * u

    @pl.when(j < n_hist)
    def _():
        cbuf[...] += wj_ref[0] * buf_ref[0]

    @pl.when(jnp.logical_and(j >= 1, j < n_hist))
    def _():
        nb_ref[0] = buf_ref[0]

    @pl.when(j == n_hist)
    def _():
        nb_ref[0] = ubuf[...]
        h_ref[...] = _conv_post(x_ref[...], cbuf[...], lng_ref[...], lnb_ref[...], w2_ref[...], b2_ref[...])


def _conv_step(x, buf_t, g, w1, b1, wdw, bdw, lng, lnb, w2, b2):
    R, D = x.shape
    n_hist = buf_t.shape[0]
    last = n_hist - 1
    return pl.pallas_call(
        _conv_step_kernel,
        grid=(n_hist + 1,),
        in_specs=[
            _const_spec((R, D)),
            pl.BlockSpec((1, R, D), lambda j: (jnp.minimum(j, last), 0, 0)),
            pl.BlockSpec((1, 1, D), lambda j: (jnp.minimum(j, last), 0, 0)),
            _const_spec((1, D)),
            _const_spec((1, D)), _const_spec((D, 2 * D)), _const_spec((1, 2 * D)),
            _const_spec((1, D)), _const_spec((1, D)), _const_spec((1, D)),
            _const_spec((D, D)), _const_spec((1, D)),
        ],
        out_specs=[_const_spec((R, D)), pl.BlockSpec((1, R, D), lambda j: (jnp.maximum(j - 1, 0), 0, 0))],
        out_shape=[jax.ShapeDtypeStruct((R, D), F32), jax.ShapeDtypeStruct((n_hist, R, D), F32)],
        scratch_shapes=[pltpu.VMEM((R, D), F32), pltpu.VMEM((R, D), F32)],
        compiler_params=_params(("arbitrary",)),
        name="conv_step",
    )(x, buf_t, wdw[:n_hist, None, :], wdw[n_hist:], g, w1, b1, bdw, lng, lnb, w2, b2)


def _mlp_kernel(x_ref, g_ref, win_ref, wout_ref, gf_ref, o_ref, *, final):
    x = x_ref[...]
    hid = jnp.maximum(_mm(_rmsnorm(x, g_ref[...]), win_ref[...]), 0.0)
    y = x + _mm(hid * hid, wout_ref[...])
    if final:
        y = _rmsnorm(y, gf_ref[...])
    o_ref[...] = y


def _mlp(x, g, win, wout, gf, *, final, tm):
    R, D = x.shape
    F = win.shape[1]
    return pl.pallas_call(
        functools.partial(_mlp_kernel, final=final),
        grid=(R // tm,),
        in_specs=[
            pl.BlockSpec((tm, D), lambda i: (i, 0)),
            _const_spec((1, D)), _const_spec((D, F)), _const_spec((F, D)), _const_spec((1, D)),
        ],
        out_specs=pl.BlockSpec((tm, D), lambda i: (i, 0)),
        out_shape=jax.ShapeDtypeStruct((R, D), F32),
        compiler_params=_params(("arbitrary",)),
        name="mlp",
    )(x, g, win, wout, gf)


def _rwkv_pre_math(hn, prev, xmix, wr, wk, wv, w0, w1, w2, a0, a1, a2, g1, g2, kk_w, ka_w):
    xx = prev - hn
    xr = hn + xx * xmix[0:1]
    xw = hn + xx * xmix[1:2]
    xk = hn + xx * xmix[2:3]
    xv = hn + xx * xmix[3:4]
    xa = hn + xx * xmix[4:5]
    xg = hn + xx * xmix[5:6]
    r = _mm(xr, wr)
    k = _mm(xk, wk)
    v = _mm(xv, wv)
    z = w0 + _mm(jnp.tanh(_mm(xw, w1)), w2)
    w_log = -(jnp.maximum(-z, 0.0) + jnp.log(1.0 + jnp.exp(-jnp.abs(z)))) - 0.5
    lw = -jnp.exp(w_log)
    iclr = jax.nn.sigmoid(a0 + _mm(_mm(xa, a1), a2))
    gate = _mm(jax.nn.sigmoid(_mm(xg, g1)), g2)
    kk = k * kk_w
    nrm = jnp.sqrt(_seg_sum(kk * kk, _head_ones()))
    kk = kk / jnp.maximum(nrm, L2_EPS)
    k2 = k * (1.0 + (iclr - 1.0) * ka_w)
    return r, lw, k2, v, -kk, kk * iclr, gate


def _rwkv_weight_specs(D, p):
    return [
        _const_spec((1, D)), _const_spec((6, D)),
        _const_spec((D, D)), _const_spec((D, D)), _const_spec((D, D)),
        _const_spec((1, D)), _const_spec(p["w1"].shape), _const_spec(p["w2"].shape),
        _const_spec((1, D)), _const_spec(p["a1"].shape), _const_spec(p["a2"].shape),
        _const_spec(p["g1"].shape), _const_spec(p["g2"].shape),
        _const_spec((1, D)), _const_spec((1, D)),
    ]


def _rwkv_weight_args(p):
    return (p["gn"], p["xmix"], p["wr"], p["wk"], p["wv"], p["w0"], p["w1"], p["w2"],
            p["a0"], p["a1"], p["a2"], p["g1"], p["g2"], p["kk"], p["ka"])


def _rwkv_pre_step_kernel(h_ref, prev_ref, gn_ref, xmix_ref, wr_ref, wk_ref, wv_ref, w0_ref, w1_ref, w2_ref,
                          a0_ref, a1_ref, a2_ref, g1_ref, g2_ref, kk_ref, ka_ref,
                          r_ref, lw_ref, k_ref, v_ref, a_ref, b_ref, g_ref, sh_ref):
    hn = _rmsnorm(h_ref[...], gn_ref[...])
    outs = _rwkv_pre_math(hn, prev_ref[...], xmix_ref[...], wr_ref[...], wk_ref[...], wv_ref[...], w0_ref[...],
                          w1_ref[...], w2_ref[...], a0_ref[...], a1_ref[...], a2_ref[...], g1_ref[...],
                          g2_ref[...], kk_ref[...], ka_ref[...])
    for o_ref, o in zip((r_ref, lw_ref, k_ref, v_ref, a_ref, b_ref, g_ref), outs):
        o_ref[...] = o
    sh_ref[...] = hn


def _rwkv_pre_step(h, prev, p):
    R, D = h.shape
    full = _const_spec((R, D))
    return pl.pallas_call(
        _rwkv_pre_step_kernel,
        grid=(1,),
        in_specs=[full, full] + _rwkv_weight_specs(D, p),
        out_specs=[full] * 8,
        out_shape=[jax.ShapeDtypeStruct((R, D), F32)] * 8,
        compiler_params=_params(("arbitrary",)),
        name="rwkv_pre_step",
    )(h, prev, *_rwkv_weight_args(p))


def _rwkv_post_math(y, r, k, v, g, h, gng, gnb, rk, wo):
    ones_bd = _head_ones()
    inv_n = 1.0 / HEAD
    mu = _seg_sum(y, ones_bd) * inv_n
    yc = y - mu
    var = _seg_sum(yc * yc, ones_bd) * inv_n
    yn = yc * lax.rsqrt(var + GN_EPS) * gng + gnb
    bonus = _seg_sum(r * k * rk, ones_bd) * v
    return h + _mm((yn + bonus) * g, wo)


def _wkv_chunk(r, lw, k, v, a, b, state, *, L):
    nh = state.shape[0]
    row = lax.broadcasted_iota(jnp.int32, lw.shape, 0)
    cs = lw
    s = 1
    while s < L:
        cs = cs + jnp.where(row >= s, pltpu.roll(cs, s, 0), 0.0)
        s *= 2
    gam = jnp.exp(cs)
    ginv = jnp.exp(-cs)
    g_last = gam[L - 1:L, :]
    rt_f = r * gam
    bt_f = b * ginv
    kt_f = k * ginv
    rt = rt_f.astype(BF16)
    at = (a * jnp.exp(cs - lw)).astype(BF16)
    bt = bt_f.astype(BF16)
    kt = kt_f.astype(BF16)
    bg = (bt_f * g_last).astype(BF16)
    kg = (kt_f * g_last).astype(BF16)
    vb = v.astype(BF16)

    ri = lax.broadcasted_iota(jnp.int32, (L, 2 * L), 0)
    ci = lax.broadcasted_iota(jnp.int32, (L, 2 * L), 1)
    left = ci < L
    cm = jnp.where(left, ci, ci - L)
    strict2 = ri > cm
    incl2 = ri >= cm
    eye_right = jnp.where(jnp.logical_and(ri == cm, jnp.logical_not(left)), 1.0, 0.0)
    zeros_l = jnp.zeros((L, HEAD), BF16)
    heads = range(nh)
    hs = lambda x, h: x[:, h * HEAD:(h + 1) * HEAD]
    dot = lambda x, y: jnp.dot(x, y, preferred_element_type=F32)
    cat0 = lambda x, y: jnp.concatenate([x, y], axis=0)
    cat1 = lambda x, y: jnp.concatenate([x, y], axis=1)

    P = [_dot_nt(cat0(hs(at, h), hs(rt, h)), cat0(hs(bt, h), hs(kt, h))) for h in heads]
    m_a = [jnp.where(strict2, p[:L], 0.0) for p in P]
    m_r = [jnp.where(incl2, p[L:], 0.0).astype(BF16) for p in P]
    makv = [dot(m_a[h][:, L:].astype(BF16), hs(vb, h)).astype(BF16) for h in heads]
    W = [jnp.where(left, m, 0.0) + eye_right for m in m_a]
    for _ in range(int(math.log2(L))):
        Z = [dot(W[h][:, :L].astype(BF16), W[h].astype(BF16)) for h in heads]
        W = [jnp.where(left, Z[h], W[h] + Z[h]) for h in heads]
    tfin = [w[:, L:].astype(BF16) for w in W]
    TX = [dot(tfin[h], cat1(hs(at, h), makv[h])).astype(BF16) for h in heads]
    RY = [dot(m_r[h], cat0(TX[h], cat1(zeros_l, hs(vb, h)))) for h in heads]
    rbar = [(hs(rt_f, h) + RY[h][:, :HEAD]).astype(BF16) for h in heads]
    GH = [_dot_tn(TX[h], hs(bg, h)) for h in heads]
    VK = [_dot_tn(hs(vb, h), hs(kg, h)) for h in heads]
    ys = []
    for h in heads:
        S = state[h]
        Sb = S.astype(BF16)
        ys.append(_dot_nt(rbar[h], Sb) + RY[h][:, HEAD:])
        state[h] = S * hs(g_last, h) + dot(Sb, GH[h][:HEAD].astype(BF16)) + GH[h][HEAD:] + VK[h]
    return ys


def _rwkv_seq_kernel(h_ref, sh0_ref, s0_ref, gn_ref, xmix_ref, wr_ref, wk_ref, wv_ref, w0_ref, w1_ref, w2_ref,
                     a0_ref, a1_ref, a2_ref, g1_ref, g2_ref, kk_ref, ka_ref, gng_ref, gnb_ref, rk_ref, wo_ref,
                     o_ref, sh_ref, sout_ref, carry, state, rb, lwb, kb, vb, ab, bb, gb, yb, *, tm, L):
    t = pl.program_id(1)

    @pl.when(t == 0)
    def _():
        carry[...] = sh0_ref[0]
        state[...] = s0_ref[0]

    h = h_ref[0]
    hn = _rmsnorm(h, gn_ref[...])
    row = lax.broadcasted_iota(jnp.int32, hn.shape, 0)
    prev = jnp.where(row == 0, carry[...], pltpu.roll(hn, 1, 0))
    last = hn[tm - 1:tm, :]
    carry[...] = last
    outs = _rwkv_pre_math(hn, prev, xmix_ref[...], wr_ref[...], wk_ref[...], wv_ref[...], w0_ref[...],
                          w1_ref[...], w2_ref[...], a0_ref[...], a1_ref[...], a2_ref[...], g1_ref[...],
                          g2_ref[...], kk_ref[...], ka_ref[...])
    for buf, o in zip((rb, lwb, kb, vb, ab, bb, gb), outs):
        buf[...] = o

    def chunk(c):
        rows = pl.ds(pl.multiple_of(c * L, L), L)
        ys = _wkv_chunk(rb[rows, :], lwb[rows, :], kb[rows, :], vb[rows, :], ab[rows, :], bb[rows, :], state, L=L)
        for i, y in enumerate(ys):
            yb[rows, i * HEAD:(i + 1) * HEAD] = y

    if tm == L:
        chunk(0)
    else:
        def body(c, carry_):
            chunk(c)
            return carry_
        lax.fori_loop(0, tm // L, body, 0)

    o_ref[0] = _rwkv_post_math(yb[...], rb[...], kb[...], vb[...], gb[...], h, gng_ref[...], gnb_ref[...],
                               rk_ref[...], wo_ref[...])

    @pl.when(t == pl.num_programs(1) - 1)
    def _():
        sh_ref[0] = last
        sout_ref[0] = state[...]


def _rwkv_seq(h, sh0, s0, p, post_p, *, tm, L):
    B, T, D = h.shape
    nh = D // HEAD
    shared = sh0.shape[0] == 1
    sh_map = (lambda b, t: (0, 0, 0)) if shared else (lambda b, t: (b, 0, 0))
    s_map = (lambda b, t: (0, 0, 0, 0)) if shared else (lambda b, t: (b, 0, 0, 0))
    tile = pl.BlockSpec((1, tm, D), lambda b, t: (b, t, 0))
    return pl.pallas_call(
        functools.partial(_rwkv_seq_kernel, tm=tm, L=L),
        grid=(B, T // tm),
        in_specs=([tile, pl.BlockSpec((1, 1, D), sh_map), pl.BlockSpec((1, nh, HEAD, HEAD), s_map)]
                  + _rwkv_weight_specs(D, p) + [_const_spec((1, D))] * 3 + [_const_spec((D, D))]),
        out_specs=[tile, pl.BlockSpec((1, 1, D), lambda b, t: (b, 0, 0)),
                   pl.BlockSpec((1, nh, HEAD, HEAD), lambda b, t: (b, 0, 0, 0))],
        out_shape=[jax.ShapeDtypeStruct((B, T, D), F32), jax.ShapeDtypeStruct((B, 1, D), F32),
                   jax.ShapeDtypeStruct((B, nh, HEAD, HEAD), F32)],
        scratch_shapes=[pltpu.VMEM((1, D), F32), pltpu.VMEM((nh, HEAD, HEAD), F32)]
                       + [pltpu.VMEM((tm, D), F32)] * 8,
        compiler_params=_params(("arbitrary", "arbitrary")),
        name="rwkv_seq",
    )(h, sh0, s0, *_rwkv_weight_args(p), *post_p)


def _wkv_step_kernel(r_ref, lw_ref, k_ref, v_ref, a_ref, b_ref, s_ref, y_ref, sout_ref):
    S = s_ref[0]
    sa = jnp.sum(S * a_ref[...], axis=1, keepdims=True)
    S = S * jnp.exp(lw_ref[...]) + sa * b_ref[...] + v_ref[0] * k_ref[...]
    sout_ref[0] = S
    y_ref[0] = jnp.sum(S * r_ref[...], axis=1, keepdims=True)


def _wkv_step(r, lw, k, v, a, b, s_t):
    R, D = r.shape
    nh = D // HEAD
    rows = lambda x: x.T.reshape(nh, HEAD, R)
    row_spec = pl.BlockSpec((1, HEAD, R), lambda h: (h, 0, 0))
    col_spec = pl.BlockSpec((1, HEAD, 1, R), lambda h: (h, 0, 0, 0))
    st_spec = pl.BlockSpec((1, HEAD, HEAD, R), lambda h: (h, 0, 0, 0))
    y, s_new = pl.pallas_call(
        _wkv_step_kernel,
        grid=(nh,),
        in_specs=[row_spec, row_spec, row_spec, col_spec, row_spec, row_spec, st_spec],
        out_specs=[col_spec, st_spec],
        out_shape=[jax.ShapeDtypeStruct((nh, HEAD, 1, R), F32), jax.ShapeDtypeStruct((nh, HEAD, HEAD, R), F32)],
        compiler_params=_params(("arbitrary",)),
        name="wkv_step",
    )(rows(r), rows(lw), rows(k), v.T.reshape(nh, HEAD, 1, R), rows(a), rows(b), s_t)
    return y.reshape(D, R).T, s_new


def _rwkv_post_kernel(y_ref, r_ref, k_ref, v_ref, g_ref, h_ref, gng_ref, gnb_ref, rk_ref, wo_ref, o_ref):
    o_ref[...] = _rwkv_post_math(y_ref[...], r_ref[...], k_ref[...], v_ref[...], g_ref[...], h_ref[...],
                                 gng_ref[...], gnb_ref[...], rk_ref[...], wo_ref[...])


def _rwkv_post(y, r, k, v, g, h, gng, gnb, rk, wo, *, tm):
    R, D = y.shape
    tile = pl.BlockSpec((tm, D), lambda i: (i, 0))
    return pl.pallas_call(
        _rwkv_post_kernel,
        grid=(R // tm,),
        in_specs=[tile] * 6 + [_const_spec((1, D))] * 3 + [_const_spec((D, D))],
        out_specs=tile,
        out_shape=jax.ShapeDtypeStruct((R, D), F32),
        compiler_params=_params(("arbitrary",)),
        name="rwkv_post",
    )(y, r, k, v, g, h, gng, gnb, rk, wo)


def _row_tile(rows, target):
    tm = min(rows, target)
    while rows % tm:
        tm //= 2
    return tm


def kernel(x_prompt, x_sample, state_conv, state_shift, state_wkv, meta_tokens, norm_mix, norm_mlp, norm_final, conv_w_pw1, conv_b_pw1, conv_w_dw, conv_b_dw, conv_ln_g, conv_ln_b, conv_w_pw2, conv_b_pw2, rwkv_x_mix, rwkv_w_r, rwkv_w_k, rwkv_w_v, rwkv_w_o, rwkv_w0, rwkv_w1, rwkv_w2, rwkv_a0, rwkv_a1, rwkv_a2, rwkv_g1, rwkv_g2, rwkv_k_k, rwkv_k_a, rwkv_r_k, rwkv_gn_g, rwkv_gn_b, w_mlp_in, w_mlp_out):
    B, T, D = x_prompt.shape
    SB = x_sample.shape[0]
    nh = D // HEAD
    depth = norm_mix.shape[0]
    assert depth == 2 and x_sample.shape[1] == 1 and D % MXU_TILE == 0

    row = lambda x: x.reshape(1, -1).astype(F32)
    bf = lambda x: x.astype(BF16)

    conv_p = (row(norm_mix[0]), bf(conv_w_pw1[0]), row(conv_b_pw1[0]), conv_w_dw[0], row(conv_b_dw[0]),
              row(conv_ln_g[0]), row(conv_ln_b[0]), bf(conv_w_pw2[0]), row(conv_b_pw2[0]))
    rwkv_p = dict(gn=row(norm_mix[1]), xmix=rwkv_x_mix[0], wr=bf(rwkv_w_r[0]), wk=bf(rwkv_w_k[0]),
                  wv=bf(rwkv_w_v[0]), w0=row(rwkv_w0[0]), w1=bf(rwkv_w1[0]), w2=bf(rwkv_w2[0]),
                  a0=row(rwkv_a0[0]), a1=bf(rwkv_a1[0]), a2=bf(rwkv_a2[0]), g1=bf(rwkv_g1[0]),
                  g2=bf(rwkv_g2[0]), kk=row(rwkv_k_k[0]), ka=row(rwkv_k_a[0]))
    post_p = (row(rwkv_gn_g[0]), row(rwkv_gn_b[0]), row(rwkv_r_k[0]), bf(rwkv_w_o[0]))
    mlp_p = [(row(norm_mlp[i]), bf(w_mlp_in[i]), bf(w_mlp_out[i])) for i in range(depth)]
    gf = row(norm_final)

    def mlp(x2d, i, final):
        return _mlp(x2d, *mlp_p[i], gf, final=final, tm=_row_tile(x2d.shape[0], 256))

    def seq_trunk(x, buf0, sh0, s0, need_out):
        b, t, _ = x.shape
        h, nb = _conv_seq(x, buf0, *conv_p, tm=_row_tile(t, 256))
        h = mlp(h.reshape(b * t, D), 0, False).reshape(b, t, D)
        h, sh, s1 = _rwkv_seq(h, sh0, s0, rwkv_p, post_p, tm=_row_tile(t, 256), L=_row_tile(t, 64))
        if not need_out:
            return None, nb, sh, s1
        return mlp(h.reshape(b * t, D), 1, True).reshape(b, t, D), nb, sh, s1

    zero_buf = jnp.zeros((1, BUF_PAD, D), F32)
    zero_sh = jnp.zeros((1, 1, D), F32)
    zero_s = jnp.zeros((1, nh, HEAD, HEAD), F32)
    _, m_buf, m_sh, m_s = seq_trunk(meta_tokens.astype(F32)[None], zero_buf, zero_sh, zero_s, False)

    y_prompt, p_buf, p_sh, p_s = seq_trunk(x_prompt, m_buf, m_sh, m_s, True)
    conv_prompt = p_buf[:, BUF_PAD - CONV_BUF:][None]
    shift_prompt = p_sh.reshape(1, B, D)
    wkv_prompt = p_s[None].astype(state_wkv.dtype)

    xs = x_sample.reshape(SB, D)
    h, hist_t = _conv_step(xs, jnp.swapaxes(state_conv[0], 0, 1), *conv_p)
    conv_sample = jnp.swapaxes(hist_t, 0, 1)[None]
    h = mlp(h, 0, False)
    r, lw, k, v, a, bb, g, sh = _rwkv_pre_step(h, state_shift[0], rwkv_p)
    y, s_t = _wkv_step(r, lw, k, v, a, bb, jnp.transpose(state_wkv[0].astype(F32), (1, 2, 3, 0)))
    s_new = jnp.transpose(s_t, (3, 0, 1, 2))
    h = _rwkv_post(y, r, k, v, g, h, *post_p, tm=SB)
    y_sample = mlp(h, 1, True).reshape(SB, 1, D)
    shift_sample = sh[None]
    wkv_sample = s_new[None].astype(state_wkv.dtype)

    return (y_prompt, y_sample, conv_prompt, shift_prompt, wkv_prompt,
            conv_sample, shift_sample, wkv_sample)
```

```python
import functools
import math

import jax
import jax.numpy as jnp
from jax import lax
from jax.experimental import pallas as pl
from jax.experimental.pallas import tpu as pltpu

F32 = jnp.float32
BF16 = jnp.bfloat16

RMS_EPS = 1e-6
LN_EPS = 1e-5
GN_EPS = 64e-5
L2_EPS = 1e-12
HEAD = 64
CONV_W = 31
CONV_BUF = CONV_W - 1
BUF_PAD = 32
CONV_SUBTILE = 128
MXU_TILE = 256
LANE = 128
SUBLANE = 8
VMEM_LIMIT = 52 * 1024 * 1024


def _mm(a, w):
    return jnp.dot(a.astype(BF16), w, preferred_element_type=F32)


def _dot_nt(a, b):
    return lax.dot_general(a, b, (((1,), (1,)), ((), ())), preferred_element_type=F32)


def _dot_tn(a, b):
    return lax.dot_general(a, b, (((0,), (0,)), ((), ())), preferred_element_type=F32)


def _rmsnorm(x, g):
    ms = jnp.mean(x * x, axis=-1, keepdims=True)
    return x * lax.rsqrt(ms + RMS_EPS) * g


def _head_ones():
    r = lax.broadcasted_iota(jnp.int32, (MXU_TILE, MXU_TILE), 0) // HEAD
    c = lax.broadcasted_iota(jnp.int32, (MXU_TILE, MXU_TILE), 1) // HEAD
    return jnp.where(r == c, 1.0, 0.0).astype(BF16)


def _seg_sum(x, ones_bd):
    hi = x.astype(BF16)
    lo = (x - hi.astype(F32)).astype(BF16)
    outs = []
    for c in range(x.shape[-1] // MXU_TILE):
        sl = slice(c * MXU_TILE, (c + 1) * MXU_TILE)
        s = jnp.dot(hi[:, sl], ones_bd, preferred_element_type=F32)
        s = s + jnp.dot(lo[:, sl], ones_bd, preferred_element_type=F32)
        outs.append(s)
    return jnp.concatenate(outs, axis=-1)


def _const_spec(shape):
    nd = len(shape)
    return pl.BlockSpec(shape, lambda *_: (0,) * nd)


def _params(sem):
    return pltpu.CompilerParams(dimension_semantics=sem, vmem_limit_bytes=VMEM_LIMIT)


def _conv_pre(x, g, w1, b1):
    d = x.shape[-1]
    u = _mm(_rmsnorm(x, g), w1) + b1
    return u[:, :d] * jax.nn.sigmoid(u[:, d:])


def _conv_post(x, c, lng, lnb, w2, b2):
    mu = jnp.mean(c, axis=-1, keepdims=True)
    cc = c - mu
    var = jnp.mean(cc * cc, axis=-1, keepdims=True)
    cn = cc * lax.rsqrt(var + LN_EPS) * lng + lnb
    return x + _mm(cn * jax.nn.sigmoid(cn), w2) + b2


def _conv_seq_kernel(x_ref, buf0_ref, g_ref, w1_ref, b1_ref, wdw_ref, bdw_ref, lng_ref, lnb_ref,
                     w2_ref, b2_ref, h_ref, nb_ref, ubuf, cbuf, shl, *, tm, ts):
    t = pl.program_id(1)
    d = x_ref.shape[-1]

    @pl.when(t == 0)
    def _():
        ubuf[0:BUF_PAD, :] = buf0_ref[0]

    def pre(i):
        rows = slice(i * ts, (i + 1) * ts)
        ubuf[BUF_PAD + i * ts:BUF_PAD + (i + 1) * ts, :] = _conv_pre(x_ref[0, rows, :], g_ref[...], w1_ref[...],
                                                                    b1_ref[...])

    off = BUF_PAD - CONV_BUF
    span = ts + BUF_PAD - SUBLANE

    def conv(i):
        r0 = i * ts
        for l in range(d // LANE):
            ls = slice(l * LANE, (l + 1) * LANE)
            for p in range(1, SUBLANE):
                shl[p - 1] = ubuf[r0 + p:r0 + p + span, ls]
            acc = jnp.broadcast_to(bdw_ref[:, ls], (ts, LANE))
            for j in range(CONV_W):
                q, p = divmod(off + j, SUBLANE)
                if p == 0:
                    src = ubuf[r0 + SUBLANE * q:r0 + SUBLANE * q + ts, ls]
                else:
                    src = shl[p - 1, SUBLANE * q:SUBLANE * q + ts, :]
                acc = acc + wdw_ref[j:j + 1, ls] * src
            cbuf[r0:r0 + ts, ls] = acc

    def post(i):
        rows = slice(i * ts, (i + 1) * ts)
        h_ref[0, rows, :] = _conv_post(x_ref[0, rows, :], cbuf[rows, :], lng_ref[...], lnb_ref[...], w2_ref[...],
                                       b2_ref[...])

    n_sub = tm // ts
    pre(0)
    for i in range(n_sub):
        if i + 1 < n_sub:
            pre(i + 1)
        conv(i)
        post(i)
    tail = ubuf[tm:tm + BUF_PAD, :]
    ubuf[0:BUF_PAD, :] = tail

    @pl.when(t == pl.num_programs(1) - 1)
    def _():
        nb_ref[0] = tail


def _conv_seq(x, buf0, g, w1, b1, wdw, bdw, lng, lnb, w2, b2, *, tm):
    B, T, D = x.shape
    nt = T // tm
    ts = min(tm, CONV_SUBTILE)
    shared = buf0.shape[0] == 1
    buf_map = (lambda b, t: (0, 0, 0)) if shared else (lambda b, t: (b, 0, 0))
    return pl.pallas_call(
        functools.partial(_conv_seq_kernel, tm=tm, ts=ts),
        grid=(B, nt),
        in_specs=[
            pl.BlockSpec((1, tm, D), lambda b, t: (b, t, 0)),
            pl.BlockSpec((1, BUF_PAD, D), buf_map),
            _const_spec((1, D)), _const_spec((D, 2 * D)), _const_spec((1, 2 * D)),
            _const_spec((CONV_W, D)), _const_spec((1, D)), _const_spec((1, D)), _const_spec((1, D)),
            _const_spec((D, D)), _const_spec((1, D)),
        ],
        out_specs=[
            pl.BlockSpec((1, tm, D), lambda b, t: (b, t, 0)),
            pl.BlockSpec((1, BUF_PAD, D), lambda b, t: (b, 0, 0)),
        ],
        out_shape=[jax.ShapeDtypeStruct((B, T, D), F32), jax.ShapeDtypeStruct((B, BUF_PAD, D), F32)],
        scratch_shapes=[pltpu.VMEM((BUF_PAD + tm, D), F32), pltpu.VMEM((tm, D), F32),
                        pltpu.VMEM((SUBLANE - 1, ts + BUF_PAD - SUBLANE, LANE), F32)],
        compiler_params=_params(("arbitrary", "arbitrary")),
        name="conv_seq",
    )(x, buf0, g, w1, b1, wdw, bdw, lng, lnb, w2, b2)


def _conv_step_kernel(x_ref, buf_ref, wj_ref, wlast_ref, g_ref, w1_ref, b1_ref, bdw_ref, lng_ref, lnb_ref,
                      w2_ref, b2_ref, h_ref, nb_ref, ubuf, cbuf):
    j = pl.program_id(0)
    n_hist = pl.num_programs(0) - 1

    @pl.when(j == 0)
    def _():
        u = _conv_pre(x_ref[...], g_ref[...], w1_ref[...], b1_ref[...])
        ubuf[...] = u
        cbuf[...] = bdw_ref[...] + wlast_ref[...] * u

    @pl.when(j < n_hist)
    def _():
        cbuf[...] += wj_ref[0] * buf_ref[0]

    @pl.when(jnp.logical_and(j >= 1, j < n_hist))
    def _():
        nb_ref[0] = buf_ref[0]

    @pl.when(j == n_hist)
    def _():
        nb_ref[0] = ubuf[...]
        h_ref[...] = _conv_post(x_ref[...], cbuf[...], lng_ref[...], lnb_ref[...], w2_ref[...], b2_ref[...])


def _conv_step(x, buf_t, g, w1, b1, wdw, bdw, lng, lnb, w2, b2):
    R, D = x.shape
    n_hist = buf_t.shape[0]
    last = n_hist - 1
    return pl.pallas_call(
        _conv_step_kernel,
        grid=(n_hist + 1,),
        in_specs=[
            _const_spec((R, D)),
            pl.BlockSpec((1, R, D), lambda j: (jnp.minimum(j, last), 0, 0)),
            pl.BlockSpec((1, 1, D), lambda j: (jnp.minimum(j, last), 0, 0)),
            _const_spec((1, D)),
            _const_spec((1, D)), _const_spec((D, 2 * D)), _const_spec((1, 2 * D)),
            _const_spec((1, D)), _const_spec((1, D)), _const_spec((1, D)),
            _const_spec((D, D)), _const_spec((1, D)),
        ],
        out_specs=[_const_spec((R, D)), pl.BlockSpec((1, R, D), lambda j: (jnp.maximum(j - 1, 0), 0, 0))],
        out_shape=[jax.ShapeDtypeStruct((R, D), F32), jax.ShapeDtypeStruct((n_hist, R, D), F32)],
        scratch_shapes=[pltpu.VMEM((R, D), F32), pltpu.VMEM((R, D), F32)],
        compiler_params=_params(("arbitrary",)),
        name="conv_step",
    )(x, buf_t, wdw[:n_hist, None, :], wdw[n_hist:], g, w1, b1, bdw, lng, lnb, w2, b2)


def _mlp_kernel(x_ref, g_ref, win_ref, wout_ref, gf_ref, o_ref, *, final):
    x = x_ref[...]
    hid = jnp.maximum(_mm(_rmsnorm(x, g_ref[...]), win_ref[...]), 0.0)
    y = x + _mm(hid * hid, wout_ref[...])
    if final:
        y = _rmsnorm(y, gf_ref[...])
    o_ref[...] = y


def _mlp(x, g, win, wout, gf, *, final, tm):
    R, D = x.shape
    F = win.shape[1]
    return pl.pallas_call(
        functools.partial(_mlp_kernel, final=final),
        grid=(R // tm,),
        in_specs=[
            pl.BlockSpec((tm, D), lambda i: (i, 0)),
            _const_spec((1, D)), _const_spec((D, F)), _const_spec((F, D)), _const_spec((1, D)),
        ],
        out_specs=pl.BlockSpec((tm, D), lambda i: (i, 0)),
        out_shape=jax.ShapeDtypeStruct((R, D), F32),
        compiler_params=_params(("arbitrary",)),
        name="mlp",
    )(x, g, win, wout, gf)


def _rwkv_pre_math(hn, prev, xmix, wr, wk, wv, w0, w1, w2, a0, a1, a2, g1, g2, kk_w, ka_w):
    xx = prev - hn
    xr = hn + xx * xmix[0:1]
    xw = hn + xx * xmix[1:2]
    xk = hn + xx * xmix[2:3]
    xv = hn + xx * xmix[3:4]
    xa = hn + xx * xmix[4:5]
    xg = hn + xx * xmix[5:6]
    r = _mm(xr, wr)
    k = _mm(xk, wk)
    v = _mm(xv, wv)
    z = w0 + _mm(jnp.tanh(_mm(xw, w1)), w2)
    w_log = -(jnp.maximum(-z, 0.0) + jnp.log(1.0 + jnp.exp(-jnp.abs(z)))) - 0.5
    lw = -jnp.exp(w_log)
    iclr = jax.nn.sigmoid(a0 + _mm(_mm(xa, a1), a2))
    gate = _mm(jax.nn.sigmoid(_mm(xg, g1)), g2)
    kk = k * kk_w
    nrm = jnp.sqrt(_seg_sum(kk * kk, _head_ones()))
    kk = kk / jnp.maximum(nrm, L2_EPS)
    k2 = k * (1.0 + (iclr - 1.0) * ka_w)
    return r, lw, k2, v, -kk, kk * iclr, gate


def _rwkv_weight_specs(D, p):
    return [
        _const_spec((1, D)), _const_spec((6, D)),
        _const_spec((D, D)), _const_spec((D, D)), _const_spec((D, D)),
        _const_spec((1, D)), _const_spec(p["w1"].shape), _const_spec(p["w2"].shape),
        _const_spec((1, D)), _const_spec(p["a1"].shape), _const_spec(p["a2"].shape),
        _const_spec(p["g1"].shape), _const_spec(p["g2"].shape),
        _const_spec((1, D)), _const_spec((1, D)),
    ]


def _rwkv_weight_args(p):
    return (p["gn"], p["xmix"], p["wr"], p["wk"], p["wv"], p["w0"], p["w1"], p["w2"],
            p["a0"], p["a1"], p["a2"], p["g1"], p["g2"], p["kk"], p["ka"])


def _rwkv_pre_step_kernel(h_ref, prev_ref, gn_ref, xmix_ref, wr_ref, wk_ref, wv_ref, w0_ref, w1_ref, w2_ref,
                          a0_ref, a1_ref, a2_ref, g1_ref, g2_ref, kk_ref, ka_ref,
                          r_ref, lw_ref, k_ref, v_ref, a_ref, b_ref, g_ref, sh_ref):
    hn = _rmsnorm(h_ref[...], gn_ref[...])
    outs = _rwkv_pre_math(hn, prev_ref[...], xmix_ref[...], wr_ref[...], wk_ref[...], wv_ref[...], w0_ref[...],
                          w1_ref[...], w2_ref[...], a0_ref[...], a1_ref[...], a2_ref[...], g1_ref[...],
                          g2_ref[...], kk_ref[...], ka_ref[...])
    for o_ref, o in zip((r_ref, lw_ref, k_ref, v_ref, a_ref, b_ref, g_ref), outs):
        o_ref[...] = o
    sh_ref[...] = hn


def _rwkv_pre_step(h, prev, p):
    R, D = h.shape
    full = _const_spec((R, D))
    return pl.pallas_call(
        _rwkv_pre_step_kernel,
        grid=(1,),
        in_specs=[full, full] + _rwkv_weight_specs(D, p),
        out_specs=[full] * 8,
        out_shape=[jax.ShapeDtypeStruct((R, D), F32)] * 8,
        compiler_params=_params(("arbitrary",)),
        name="rwkv_pre_step",
    )(h, prev, *_rwkv_weight_args(p))


def _rwkv_post_math(y, r, k, v, g, h, gng, gnb, rk, wo):
    ones_bd = _head_ones()
    inv_n = 1.0 / HEAD
    mu = _seg_sum(y, ones_bd) * inv_n
    yc = y - mu
    var = _seg_sum(yc * yc, ones_bd) * inv_n
    yn = yc * lax.rsqrt(var + GN_EPS) * gng + gnb
    bonus = _seg_sum(r * k * rk, ones_bd) * v
    return h + _mm((yn + bonus) * g, wo)


def _wkv_chunk(r, lw, k, v, a, b, state, *, L):
    nh = state.shape[0]
    row = lax.broadcasted_iota(jnp.int32, lw.shape, 0)
    cs = lw
    s = 1
    while s < L:
        cs = cs + jnp.where(row >= s, pltpu.roll(cs, s, 0), 0.0)
        s *= 2
    gam = jnp.exp(cs)
    ginv = jnp.exp(-cs)
    g_last = gam[L - 1:L, :]
    rt_f = r * gam
    bt_f = b * ginv
    kt_f = k * ginv
    rt = rt_f.astype(BF16)
    at = (a * jnp.exp(cs - lw)).astype(BF16)
    bt = bt_f.astype(BF16)
    kt = kt_f.astype(BF16)
    bg = (bt_f * g_last).astype(BF16)
    kg = (kt_f * g_last).astype(BF16)
    vb = v.astype(BF16)

    ri = lax.broadcasted_iota(jnp.int32, (L, 2 * L), 0)
    ci = lax.broadcasted_iota(jnp.int32, (L, 2 * L), 1)
    left = ci < L
    cm = jnp.where(left, ci, ci - L)
    strict2 = ri > cm
    incl2 = ri >= cm
    eye_right = jnp.where(jnp.logical_and(ri == cm, jnp.logical_not(left)), 1.0, 0.0)
    zeros_l = jnp.zeros((L, HEAD), BF16)
    heads = range(nh)
    hs = lambda x, h: x[:, h * HEAD:(h + 1) * HEAD]
    dot = lambda x, y: jnp.dot(x, y, preferred_element_type=F32)
    cat0 = lambda x, y: jnp.concatenate([x, y], axis=0)
    cat1 = lambda x, y: jnp.concatenate([x, y], axis=1)

    bk_t = jnp.transpose(cat0(bt_f, kt_f)).astype(BF16)
    P = [dot(cat0(hs(at, h), hs(rt, h)), bk_t[h * HEAD:(h + 1) * HEAD, :]) for h in heads]
    m_a = [jnp.where(strict2, p[:L], 0.0) for p in P]
    m_r = [jnp.where(incl2, p[L:], 0.0).astype(BF16) for p in P]
    makv = [dot(m_a[h][:, L:].astype(BF16), hs(vb, h)).astype(BF16) for h in heads]
    W = [jnp.where(left, m, 0.0) + eye_right for m in m_a]
    for _ in range(int(math.log2(L))):
        Z = [dot(W[h][:, :L].astype(BF16), W[h].astype(BF16)) for h in heads]
        W = [jnp.where(left, Z[h], W[h] + Z[h]) for h in heads]
    tfin = [w[:, L:].astype(BF16) for w in W]
    TX = [dot(tfin[h], cat1(hs(at, h), makv[h])).astype(BF16) for h in heads]
    XV = [cat0(TX[h], cat1(zeros_l, hs(vb, h))) for h in heads]
    RY = [dot(m_r[h], XV[h]) for h in heads]
    rbar = [(hs(rt_f, h) + RY[h][:, :HEAD]).astype(BF16) for h in heads]
    GH = [_dot_tn(XV[h], cat0(hs(bg, h), hs(kg, h))) for h in heads]
    ys = []
    for h in heads:
        S = state[h]
        Sb = S.astype(BF16)
        ys.append(_dot_nt(rbar[h], Sb) + RY[h][:, HEAD:])
        state[h] = S * hs(g_last, h) + dot(Sb, GH[h][:HEAD].astype(BF16)) + GH[h][HEAD:]
    return ys


def _rwkv_seq_kernel(h_ref, sh0_ref, s0_ref, gn_ref, xmix_ref, wr_ref, wk_ref, wv_ref, w0_ref, w1_ref, w2_ref,
                     a0_ref, a1_ref, a2_ref, g1_ref, g2_ref, kk_ref, ka_ref, gng_ref, gnb_ref, rk_ref, wo_ref,
                     o_ref, sh_ref, sout_ref, carry, state, rb, lwb, kb, vb, ab, bb, gb, yb, *, tm, L):
    t = pl.program_id(1)

    @pl.when(t == 0)
    def _():
        carry[...] = sh0_ref[0]
        state[...] = s0_ref[0]

    h = h_ref[0]
    hn = _rmsnorm(h, gn_ref[...])
    row = lax.broadcasted_iota(jnp.int32, hn.shape, 0)
    prev = jnp.where(row == 0, carry[...], pltpu.roll(hn, 1, 0))
    last = hn[tm - 1:tm, :]
    carry[...] = last
    outs = _rwkv_pre_math(hn, prev, xmix_ref[...], wr_ref[...], wk_ref[...], wv_ref[...], w0_ref[...],
                          w1_ref[...], w2_ref[...], a0_ref[...], a1_ref[...], a2_ref[...], g1_ref[...],
                          g2_ref[...], kk_ref[...], ka_ref[...])
    for buf, o in zip((rb, lwb, kb, vb, ab, bb, gb), outs):
        buf[...] = o

    def chunk(c):
        rows = pl.ds(pl.multiple_of(c * L, L), L)
        ys = _wkv_chunk(rb[rows, :], lwb[rows, :], kb[rows, :], vb[rows, :], ab[rows, :], bb[rows, :], state, L=L)
        for i, y in enumerate(ys):
            yb[rows, i * HEAD:(i + 1) * HEAD] = y

    if tm == L:
        chunk(0)
    else:
        def body(c, carry_):
            chunk(c)
            return carry_
        lax.fori_loop(0, tm // L, body, 0, unroll=True)

    o_ref[0] = _rwkv_post_math(yb[...], rb[...], kb[...], vb[...], gb[...], h, gng_ref[...], gnb_ref[...],
                               rk_ref[...], wo_ref[...])

    @pl.when(t == pl.num_programs(1) - 1)
    def _():
        sh_ref[0] = last
        sout_ref[0] = state[...]


def _rwkv_seq(h, sh0, s0, p, post_p, *, tm, L):
    B, T, D = h.shape
    nh = D // HEAD
    shared = sh0.shape[0] == 1
    sh_map = (lambda b, t: (0, 0, 0)) if shared else (lambda b, t: (b, 0, 0))
    s_map = (lambda b, t: (0, 0, 0, 0)) if shared else (lambda b, t: (b, 0, 0, 0))
    tile = pl.BlockSpec((1, tm, D), lambda b, t: (b, t, 0))
    return pl.pallas_call(
        functools.partial(_rwkv_seq_kernel, tm=tm, L=L),
        grid=(B, T // tm),
        in_specs=([tile, pl.BlockSpec((1, 1, D), sh_map), pl.BlockSpec((1, nh, HEAD, HEAD), s_map)]
                  + _rwkv_weight_specs(D, p) + [_const_spec((1, D))] * 3 + [_const_spec((D, D))]),
        out_specs=[tile, pl.BlockSpec((1, 1, D), lambda b, t: (b, 0, 0)),
                   pl.BlockSpec((1, nh, HEAD, HEAD), lambda b, t: (b, 0, 0, 0))],
        out_shape=[jax.ShapeDtypeStruct((B, T, D), F32), jax.ShapeDtypeStruct((B, 1, D), F32),
                   jax.ShapeDtypeStruct((B, nh, HEAD, HEAD), F32)],
        scratch_shapes=[pltpu.VMEM((1, D), F32), pltpu.VMEM((nh, HEAD, HEAD), F32)]
                       + [pltpu.VMEM((tm, D), F32)] * 8,
        compiler_params=_params(("arbitrary", "arbitrary")),
        name="rwkv_seq",
    )(h, sh0, s0, *_rwkv_weight_args(p), *post_p)


def _wkv_step_kernel(r_ref, lw_ref, k_ref, v_ref, a_ref, b_ref, s_ref, y_ref, sout_ref):
    S = s_ref[0]
    sa = jnp.sum(S * a_ref[...], axis=1, keepdims=True)
    S = S * jnp.exp(lw_ref[...]) + sa * b_ref[...] + v_ref[0] * k_ref[...]
    sout_ref[0] = S
    y_ref[0] = jnp.sum(S * r_ref[...], axis=1, keepdims=True)


def _wkv_step(r, lw, k, v, a, b, s_t):
    R, D = r.shape
    nh = D // HEAD
    rows = lambda x: x.T.reshape(nh, HEAD, R)
    row_spec = pl.BlockSpec((1, HEAD, R), lambda h: (h, 0, 0))
    col_spec = pl.BlockSpec((1, HEAD, 1, R), lambda h: (h, 0, 0, 0))
    st_spec = pl.BlockSpec((1, HEAD, HEAD, R), lambda h: (h, 0, 0, 0))
    y, s_new = pl.pallas_call(
        _wkv_step_kernel,
        grid=(nh,),
        in_specs=[row_spec, row_spec, row_spec, col_spec, row_spec, row_spec, st_spec],
        out_specs=[col_spec, st_spec],
        out_shape=[jax.ShapeDtypeStruct((nh, HEAD, 1, R), F32), jax.ShapeDtypeStruct((nh, HEAD, HEAD, R), F32)],
        compiler_params=_params(("arbitrary",)),
        name="wkv_step",
    )(rows(r), rows(lw), rows(k), v.T.reshape(nh, HEAD, 1, R), rows(a), rows(b), s_t)
    return y.reshape(D, R).T, s_new


def _rwkv_post_kernel(y_ref, r_ref, k_ref, v_ref, g_ref, h_ref, gng_ref, gnb_ref, rk_ref, wo_ref, o_ref):
    o_ref[...] = _rwkv_post_math(y_ref[...], r_ref[...], k_ref[...], v_ref[...], g_ref[...], h_ref[...],
                                 gng_ref[...], gnb_ref[...], rk_ref[...], wo_ref[...])


def _rwkv_post(y, r, k, v, g, h, gng, gnb, rk, wo, *, tm):
    R, D = y.shape
    tile = pl.BlockSpec((tm, D), lambda i: (i, 0))
    return pl.pallas_call(
        _rwkv_post_kernel,
        grid=(R // tm,),
        in_specs=[tile] * 6 + [_const_spec((1, D))] * 3 + [_const_spec((D, D))],
        out_specs=tile,
        out_shape=jax.ShapeDtypeStruct((R, D), F32),
        compiler_params=_params(("arbitrary",)),
        name="rwkv_post",
    )(y, r, k, v, g, h, gng, gnb, rk, wo)


def _row_tile(rows, target):
    tm = min(rows, target)
    while rows % tm:
        tm //= 2
    return tm


def kernel(x_prompt, x_sample, state_conv, state_shift, state_wkv, meta_tokens, norm_mix, norm_mlp, norm_final, conv_w_pw1, conv_b_pw1, conv_w_dw, conv_b_dw, conv_ln_g, conv_ln_b, conv_w_pw2, conv_b_pw2, rwkv_x_mix, rwkv_w_r, rwkv_w_k, rwkv_w_v, rwkv_w_o, rwkv_w0, rwkv_w1, rwkv_w2, rwkv_a0, rwkv_a1, rwkv_a2, rwkv_g1, rwkv_g2, rwkv_k_k, rwkv_k_a, rwkv_r_k, rwkv_gn_g, rwkv_gn_b, w_mlp_in, w_mlp_out):
    B, T, D = x_prompt.shape
    SB = x_sample.shape[0]
    nh = D // HEAD
    depth = norm_mix.shape[0]
    assert depth == 2 and x_sample.shape[1] == 1 and D % MXU_TILE == 0

    row = lambda x: x.reshape(1, -1).astype(F32)
    bf = lambda x: x.astype(BF16)

    conv_p = (row(norm_mix[0]), bf(conv_w_pw1[0]), row(conv_b_pw1[0]), conv_w_dw[0], row(conv_b_dw[0]),
              row(conv_ln_g[0]), row(conv_ln_b[0]), bf(conv_w_pw2[0]), row(conv_b_pw2[0]))
    rwkv_p = dict(gn=row(norm_mix[1]), xmix=rwkv_x_mix[0], wr=bf(rwkv_w_r[0]), wk=bf(rwkv_w_k[0]),
                  wv=bf(rwkv_w_v[0]), w0=row(rwkv_w0[0]), w1=bf(rwkv_w1[0]), w2=bf(rwkv_w2[0]),
                  a0=row(rwkv_a0[0]), a1=bf(rwkv_a1[0]), a2=bf(rwkv_a2[0]), g1=bf(rwkv_g1[0]),
                  g2=bf(rwkv_g2[0]), kk=row(rwkv_k_k[0]), ka=row(rwkv_k_a[0]))
    post_p = (row(rwkv_gn_g[0]), row(rwkv_gn_b[0]), row(rwkv_r_k[0]), bf(rwkv_w_o[0]))
    mlp_p = [(row(norm_mlp[i]), bf(w_mlp_in[i]), bf(w_mlp_out[i])) for i in range(depth)]
    gf = row(norm_final)

    def mlp(x2d, i, final):
        return _mlp(x2d, *mlp_p[i], gf, final=final, tm=_row_tile(x2d.shape[0], 256))

    def seq_trunk(x, buf0, sh0, s0, need_out):
        b, t, _ = x.shape
        h, nb = _conv_seq(x, buf0, *conv_p, tm=_row_tile(t, 512))
        h = mlp(h.reshape(b * t, D), 0, False).reshape(b, t, D)
        h, sh, s1 = _rwkv_seq(h, sh0, s0, rwkv_p, post_p, tm=_row_tile(t, 256), L=_row_tile(t, 64))
        if not need_out:
            return None, nb, sh, s1
        return mlp(h.reshape(b * t, D), 1, True).reshape(b, t, D), nb, sh, s1

    zero_buf = jnp.zeros((1, BUF_PAD, D), F32)
    zero_sh = jnp.zeros((1, 1, D), F32)
    zero_s = jnp.zeros((1, nh, HEAD, HEAD), F32)
    _, m_buf, m_sh, m_s = seq_trunk(meta_tokens.astype(F32)[None], zero_buf, zero_sh, zero_s, False)

    y_prompt, p_buf, p_sh, p_s = seq_trunk(x_prompt, m_buf, m_sh, m_s, True)
    conv_prompt = p_buf[:, BUF_PAD - CONV_BUF:][None]
    shift_prompt = p_sh.reshape(1, B, D)
    wkv_prompt = p_s[None].astype(state_wkv.dtype)

    xs = x_sample.reshape(SB, D)
    h, hist_t = _conv_step(xs, jnp.swapaxes(state_conv[0], 0, 1), *conv_p)
    conv_sample = jnp.swapaxes(hist_t, 0, 1)[None]
    h = mlp(h, 0, False)
    r, lw, k, v, a, bb, g, sh = _rwkv_pre_step(h, state_shift[0], rwkv_p)
    y, s_t = _wkv_step(r, lw, k, v, a, bb, jnp.transpose(state_wkv[0].astype(F32), (1, 2, 3, 0)))
    s_new = jnp.transpose(s_t, (3, 0, 1, 2))
    h = _rwkv_post(y, r, k, v, g, h, *post_p, tm=SB)
    y_sample = mlp(h, 1, True).reshape(SB, 1, D)
    shift_sample = sh[None]
    wkv_sample = s_new[None].astype(state_wkv.dtype)

    return (y_prompt, y_sample, conv_prompt, shift_prompt, wkv_prompt,
            conv_sample, shift_sample, wkv_sample)
```

```python
import functools
import math

import jax
import jax.numpy as jnp
from jax import lax
from jax.experimental import pallas as pl
from jax.experimental.pallas import tpu as pltpu

F32 = jnp.float32
BF16 = jnp.bfloat16

RMS_EPS = 1e-6
LN_EPS = 1e-5
GN_EPS = 64e-5
L2_EPS = 1e-12
HEAD = 64
CONV_W = 31
CONV_BUF = CONV_W - 1
BUF_PAD = 32
CONV_SUBTILE = 256
MXU_TILE = 256
LANE = 128
SUBLANE = 8
VMEM_LIMIT = 52 * 1024 * 1024


def _mm(a, w):
    return jnp.dot(a.astype(BF16), w, preferred_element_type=F32)


def _dot_nt(a, b):
    return lax.dot_general(a, b, (((1,), (1,)), ((), ())), preferred_element_type=F32)


def _dot_tn(a, b):
    return lax.dot_general(a, b, (((0,), (0,)), ((), ())), preferred_element_type=F32)


def _rmsnorm(x, g):
    ms = jnp.mean(x * x, axis=-1, keepdims=True)
    return x * lax.rsqrt(ms + RMS_EPS) * g


def _head_ones():
    r = lax.broadcasted_iota(jnp.int32, (MXU_TILE, MXU_TILE), 0) // HEAD
    c = lax.broadcasted_iota(jnp.int32, (MXU_TILE, MXU_TILE), 1) // HEAD
    return jnp.where(r == c, 1.0, 0.0).astype(BF16)


def _seg_sum(x, ones_bd):
    hi = x.astype(BF16)
    lo = (x - hi.astype(F32)).astype(BF16)
    outs = []
    for c in range(x.shape[-1] // MXU_TILE):
        sl = slice(c * MXU_TILE, (c + 1) * MXU_TILE)
        s = jnp.dot(hi[:, sl], ones_bd, preferred_element_type=F32)
        s = s + jnp.dot(lo[:, sl], ones_bd, preferred_element_type=F32)
        outs.append(s)
    return jnp.concatenate(outs, axis=-1)


def _const_spec(shape):
    nd = len(shape)
    return pl.BlockSpec(shape, lambda *_: (0,) * nd)


def _weight_spec(shape):
    nd = len(shape)
    return pl.BlockSpec(shape, lambda *_: (0,) * nd, pipeline_mode=pl.Buffered(1))


def _params(sem):
    return pltpu.CompilerParams(dimension_semantics=sem, vmem_limit_bytes=VMEM_LIMIT)


def _conv_pre(x, g, w1, b1):
    d = x.shape[-1]
    u = _mm(_rmsnorm(x, g), w1) + b1
    return u[:, :d] * jax.nn.sigmoid(u[:, d:])


def _conv_post(x, c, lng, lnb, w2, b2):
    mu = jnp.mean(c, axis=-1, keepdims=True)
    cc = c - mu
    var = jnp.mean(cc * cc, axis=-1, keepdims=True)
    cn = cc * lax.rsqrt(var + LN_EPS) * lng + lnb
    return x + _mm(cn * jax.nn.sigmoid(cn), w2) + b2


def _conv_seq_kernel(x_ref, buf0_ref, g_ref, w1_ref, b1_ref, wdw_ref, bdw_ref, lng_ref, lnb_ref,
                     w2_ref, b2_ref, gm_ref, win_ref, wout_ref, h_ref, nb_ref, ubuf, cbuf, shl, *, tm, ts):
    t = pl.program_id(1)
    d = x_ref.shape[-1]

    @pl.when(t == 0)
    def _():
        ubuf[0:BUF_PAD, :] = buf0_ref[0]

    def pre(i):
        rows = slice(i * ts, (i + 1) * ts)
        ubuf[BUF_PAD + i * ts:BUF_PAD + (i + 1) * ts, :] = _conv_pre(x_ref[0, rows, :], g_ref[...], w1_ref[...],
                                                                    b1_ref[...])

    off = BUF_PAD - CONV_BUF
    span = ts + BUF_PAD - SUBLANE

    def conv(i):
        r0 = i * ts
        for l in range(d // LANE):
            ls = slice(l * LANE, (l + 1) * LANE)
            for p in range(1, SUBLANE):
                shl[p - 1] = ubuf[r0 + p:r0 + p + span, ls]
            acc = jnp.broadcast_to(bdw_ref[:, ls], (ts, LANE))
            for j in range(CONV_W):
                q, p = divmod(off + j, SUBLANE)
                if p == 0:
                    src = ubuf[r0 + SUBLANE * q:r0 + SUBLANE * q + ts, ls]
                else:
                    src = shl[p - 1, SUBLANE * q:SUBLANE * q + ts, :]
                acc = acc + wdw_ref[j:j + 1, ls] * src
            cbuf[r0:r0 + ts, ls] = acc

    def post_mlp(i):
        rows = slice(i * ts, (i + 1) * ts)
        h = _conv_post(x_ref[0, rows, :], cbuf[rows, :], lng_ref[...], lnb_ref[...], w2_ref[...], b2_ref[...])
        h_ref[0, rows, :] = _mlp_math(h, gm_ref[...], win_ref[...], wout_ref[...])

    n_sub = tm // ts
    pre(0)
    for i in range(n_sub):
        if i + 1 < n_sub:
            pre(i + 1)
        conv(i)
        post_mlp(i)
    tail = ubuf[tm:tm + BUF_PAD, :]
    ubuf[0:BUF_PAD, :] = tail

    @pl.when(t == pl.num_programs(1) - 1)
    def _():
        nb_ref[0] = tail


def _conv_seq(x, buf0, g, w1, b1, wdw, bdw, lng, lnb, w2, b2, gm, win, wout, *, tm):
    B, T, D = x.shape
    F = win.shape[1]
    nt = T // tm
    ts = min(tm, CONV_SUBTILE)
    shared = buf0.shape[0] == 1
    buf_map = (lambda b, t: (0, 0, 0)) if shared else (lambda b, t: (b, 0, 0))
    return pl.pallas_call(
        functools.partial(_conv_seq_kernel, tm=tm, ts=ts),
        grid=(B, nt),
        in_specs=[
            pl.BlockSpec((1, tm, D), lambda b, t: (b, t, 0)),
            pl.BlockSpec((1, BUF_PAD, D), buf_map),
            _const_spec((1, D)), _weight_spec((D, 2 * D)), _const_spec((1, 2 * D)),
            _const_spec((CONV_W, D)), _const_spec((1, D)), _const_spec((1, D)), _const_spec((1, D)),
            _weight_spec((D, D)), _const_spec((1, D)),
            _const_spec((1, D)), _weight_spec((D, F)), _weight_spec((F, D)),
        ],
        out_specs=[
            pl.BlockSpec((1, tm, D), lambda b, t: (b, t, 0)),
            pl.BlockSpec((1, BUF_PAD, D), lambda b, t: (b, 0, 0)),
        ],
        out_shape=[jax.ShapeDtypeStruct((B, T, D), F32), jax.ShapeDtypeStruct((B, BUF_PAD, D), F32)],
        scratch_shapes=[pltpu.VMEM((BUF_PAD + tm, D), F32), pltpu.VMEM((tm, D), F32),
                        pltpu.VMEM((SUBLANE - 1, ts + BUF_PAD - SUBLANE, LANE), F32)],
        compiler_params=_params(("arbitrary", "arbitrary")),
        name="conv_seq",
    )(x, buf0, g, w1, b1, wdw, bdw, lng, lnb, w2, b2, gm, win, wout)


def _conv_step_kernel(x_ref, buf_ref, wj_ref, wlast_ref, g_ref, w1_ref, b1_ref, bdw_ref, lng_ref, lnb_ref,
                      w2_ref, b2_ref, h_ref, nb_ref, ubuf, cbuf):
    j = pl.program_id(0)
    n_hist = pl.num_programs(0) - 1

    @pl.when(j == 0)
    def _():
        u = _conv_pre(x_ref[...], g_ref[...], w1_ref[...], b1_ref[...])
        ubuf[...] = u
        cbuf[...] = bdw_ref[...] + wlast_ref[...] * u

    @pl.when(j < n_hist)
    def _():
        cbuf[...] += wj_ref[0] * buf_ref[0]

    @pl.when(jnp.logical_and(j >= 1, j < n_hist))
    def _():
        nb_ref[0] = buf_ref[0]

    @pl.when(j == n_hist)
    def _():
        nb_ref[0] = ubuf[...]
        h_ref[...] = _conv_post(x_ref[...], cbuf[...], lng_ref[...], lnb_ref[...], w2_ref[...], b2_ref[...])


def _conv_step(x, buf_t, g, w1, b1, wdw, bdw, lng, lnb, w2, b2):
    R, D = x.shape
    n_hist = buf_t.shape[0]
    last = n_hist - 1
    return pl.pallas_call(
        _conv_step_kernel,
        grid=(n_hist + 1,),
        in_specs=[
            _const_spec((R, D)),
            pl.BlockSpec((1, R, D), lambda j: (jnp.minimum(j, last), 0, 0)),
            pl.BlockSpec((1, 1, D), lambda j: (jnp.minimum(j, last), 0, 0)),
            _const_spec((1, D)),
            _const_spec((1, D)), _const_spec((D, 2 * D)), _const_spec((1, 2 * D)),
            _const_spec((1, D)), _const_spec((1, D)), _const_spec((1, D)),
            _const_spec((D, D)), _const_spec((1, D)),
        ],
        out_specs=[_const_spec((R, D)), pl.BlockSpec((1, R, D), lambda j: (jnp.maximum(j - 1, 0), 0, 0))],
        out_shape=[jax.ShapeDtypeStruct((R, D), F32), jax.ShapeDtypeStruct((n_hist, R, D), F32)],
        scratch_shapes=[pltpu.VMEM((R, D), F32), pltpu.VMEM((R, D), F32)],
        compiler_params=_params(("arbitrary",)),
        name="conv_step",
    )(x, buf_t, wdw[:n_hist, None, :], wdw[n_hist:], g, w1, b1, bdw, lng, lnb, w2, b2)


def _mlp_math(x, g, win, wout):
    hid = jnp.maximum(_mm(_rmsnorm(x, g), win), 0.0)
    return x + _mm(hid * hid, wout)


def _mlp_kernel(x_ref, g_ref, win_ref, wout_ref, gf_ref, o_ref, *, final):
    y = _mlp_math(x_ref[...], g_ref[...], win_ref[...], wout_ref[...])
    if final:
        y = _rmsnorm(y, gf_ref[...])
    o_ref[...] = y


def _mlp(x, g, win, wout, gf, *, final, tm):
    R, D = x.shape
    F = win.shape[1]
    return pl.pallas_call(
        functools.partial(_mlp_kernel, final=final),
        grid=(R // tm,),
        in_specs=[
            pl.BlockSpec((tm, D), lambda i: (i, 0)),
            _const_spec((1, D)), _const_spec((D, F)), _const_spec((F, D)), _const_spec((1, D)),
        ],
        out_specs=pl.BlockSpec((tm, D), lambda i: (i, 0)),
        out_shape=jax.ShapeDtypeStruct((R, D), F32),
        compiler_params=_params(("arbitrary",)),
        name="mlp",
    )(x, g, win, wout, gf)


def _rwkv_pre_math(hn, prev, xmix, wr, wk, wv, w0, w1, w2, a0, a1, a2, g1, g2, kk_w, ka_w):
    xx = prev - hn
    xr = hn + xx * xmix[0:1]
    xw = hn + xx * xmix[1:2]
    xk = hn + xx * xmix[2:3]
    xv = hn + xx * xmix[3:4]
    xa = hn + xx * xmix[4:5]
    xg = hn + xx * xmix[5:6]
    r = _mm(xr, wr)
    k = _mm(xk, wk)
    v = _mm(xv, wv)
    z = w0 + _mm(jnp.tanh(_mm(xw, w1)), w2)
    w_log = -(jnp.maximum(-z, 0.0) + jnp.log(1.0 + jnp.exp(-jnp.abs(z)))) - 0.5
    lw = -jnp.exp(w_log)
    iclr = jax.nn.sigmoid(a0 + _mm(_mm(xa, a1), a2))
    gate = _mm(jax.nn.sigmoid(_mm(xg, g1)), g2)
    kk = k * kk_w
    nrm = jnp.sqrt(_seg_sum(kk * kk, _head_ones()))
    kk = kk / jnp.maximum(nrm, L2_EPS)
    k2 = k * (1.0 + (iclr - 1.0) * ka_w)
    return r, lw, k2, v, -kk, kk * iclr, gate


def _rwkv_weight_specs(D, p):
    return [
        _const_spec((1, D)), _const_spec((6, D)),
        _const_spec((D, D)), _const_spec((D, D)), _const_spec((D, D)),
        _const_spec((1, D)), _const_spec(p["w1"].shape), _const_spec(p["w2"].shape),
        _const_spec((1, D)), _const_spec(p["a1"].shape), _const_spec(p["a2"].shape),
        _const_spec(p["g1"].shape), _const_spec(p["g2"].shape),
        _const_spec((1, D)), _const_spec((1, D)),
    ]


def _rwkv_weight_args(p):
    return (p["gn"], p["xmix"], p["wr"], p["wk"], p["wv"], p["w0"], p["w1"], p["w2"],
            p["a0"], p["a1"], p["a2"], p["g1"], p["g2"], p["kk"], p["ka"])


def _rwkv_pre_step_kernel(h_ref, prev_ref, gn_ref, xmix_ref, wr_ref, wk_ref, wv_ref, w0_ref, w1_ref, w2_ref,
                          a0_ref, a1_ref, a2_ref, g1_ref, g2_ref, kk_ref, ka_ref,
                          r_ref, lw_ref, k_ref, v_ref, a_ref, b_ref, g_ref, sh_ref):
    hn = _rmsnorm(h_ref[...], gn_ref[...])
    outs = _rwkv_pre_math(hn, prev_ref[...], xmix_ref[...], wr_ref[...], wk_ref[...], wv_ref[...], w0_ref[...],
                          w1_ref[...], w2_ref[...], a0_ref[...], a1_ref[...], a2_ref[...], g1_ref[...],
                          g2_ref[...], kk_ref[...], ka_ref[...])
    for o_ref, o in zip((r_ref, lw_ref, k_ref, v_ref, a_ref, b_ref, g_ref), outs):
        o_ref[...] = o
    sh_ref[...] = hn


def _rwkv_pre_step(h, prev, p):
    R, D = h.shape
    full = _const_spec((R, D))
    return pl.pallas_call(
        _rwkv_pre_step_kernel,
        grid=(1,),
        in_specs=[full, full] + _rwkv_weight_specs(D, p),
        out_specs=[full] * 8,
        out_shape=[jax.ShapeDtypeStruct((R, D), F32)] * 8,
        compiler_params=_params(("arbitrary",)),
        name="rwkv_pre_step",
    )(h, prev, *_rwkv_weight_args(p))


def _rwkv_post_math(y, r, k, v, g, h, gng, gnb, rk, wo):
    ones_bd = _head_ones()
    inv_n = 1.0 / HEAD
    mu = _seg_sum(y, ones_bd) * inv_n
    yc = y - mu
    var = _seg_sum(yc * yc, ones_bd) * inv_n
    yn = yc * lax.rsqrt(var + GN_EPS) * gng + gnb
    bonus = _seg_sum(r * k * rk, ones_bd) * v
    return h + _mm((yn + bonus) * g, wo)


def _wkv_chunk(r, lw, k, v, a, b, state, *, L):
    nh = state.shape[0]
    row = lax.broadcasted_iota(jnp.int32, lw.shape, 0)
    cs = lw
    s = 1
    while s < L:
        cs = cs + jnp.where(row >= s, pltpu.roll(cs, s, 0), 0.0)
        s *= 2
    gam = jnp.exp(cs)
    ginv = jnp.exp(-cs)
    g_last = gam[L - 1:L, :]
    rt_f = r * gam
    bt_f = b * ginv
    kt_f = k * ginv
    rt = rt_f.astype(BF16)
    at = (a * jnp.exp(cs - lw)).astype(BF16)
    bt = bt_f.astype(BF16)
    kt = kt_f.astype(BF16)
    bg = (bt_f * g_last).astype(BF16)
    kg = (kt_f * g_last).astype(BF16)
    vb = v.astype(BF16)

    ri = lax.broadcasted_iota(jnp.int32, (L, 2 * L), 0)
    ci = lax.broadcasted_iota(jnp.int32, (L, 2 * L), 1)
    left = ci < L
    cm = jnp.where(left, ci, ci - L)
    strict2 = ri > cm
    incl2 = ri >= cm
    eye_right = jnp.where(jnp.logical_and(ri == cm, jnp.logical_not(left)), 1.0, 0.0)
    zeros_l = jnp.zeros((L, HEAD), BF16)
    heads = range(nh)
    hs = lambda x, h: x[:, h * HEAD:(h + 1) * HEAD]
    dot = lambda x, y: jnp.dot(x, y, preferred_element_type=F32)
    cat0 = lambda x, y: jnp.concatenate([x, y], axis=0)
    cat1 = lambda x, y: jnp.concatenate([x, y], axis=1)

    bk_t = jnp.transpose(cat0(bt_f, kt_f)).astype(BF16)
    P = [dot(cat0(hs(at, h), hs(rt, h)), bk_t[h * HEAD:(h + 1) * HEAD, :]) for h in heads]
    m_a = [jnp.where(strict2, p[:L], 0.0) for p in P]
    m_r = [jnp.where(incl2, p[L:], 0.0).astype(BF16) for p in P]
    makv = [dot(m_a[h][:, L:].astype(BF16), hs(vb, h)).astype(BF16) for h in heads]
    W = [jnp.where(left, m, 0.0) + eye_right for m in m_a]
    for _ in range(int(math.log2(L))):
        Z = [dot(W[h][:, :L].astype(BF16), W[h].astype(BF16)) for h in heads]
        W = [jnp.where(left, Z[h], W[h] + Z[h]) for h in heads]
    tfin = [w[:, L:].astype(BF16) for w in W]
    TX = [dot(tfin[h], cat1(hs(at, h), makv[h])).astype(BF16) for h in heads]
    XV = [cat0(TX[h], cat1(zeros_l, hs(vb, h))) for h in heads]
    RY = [dot(m_r[h], XV[h]) for h in heads]
    rbar = [(hs(rt_f, h) + RY[h][:, :HEAD]).astype(BF16) for h in heads]
    GH = [_dot_tn(XV[h], cat0(hs(bg, h), hs(kg, h))) for h in heads]
    ys = []
    for h in heads:
        S = state[h]
        Sb = S.astype(BF16)
        ys.append(_dot_nt(rbar[h], Sb) + RY[h][:, HEAD:])
        state[h] = S * hs(g_last, h) + dot(Sb, GH[h][:HEAD].astype(BF16)) + GH[h][HEAD:]
    return ys


def _rwkv_seq_kernel(h_ref, sh0_ref, s0_ref, gn_ref, xmix_ref, wr_ref, wk_ref, wv_ref, w0_ref, w1_ref, w2_ref,
                     a0_ref, a1_ref, a2_ref, g1_ref, g2_ref, kk_ref, ka_ref, gng_ref, gnb_ref, rk_ref, wo_ref,
                     o_ref, sh_ref, sout_ref, carry, state, rb, lwb, kb, vb, ab, bb, gb, yb, *, tm, L):
    t = pl.program_id(1)

    @pl.when(t == 0)
    def _():
        carry[...] = sh0_ref[0]
        state[...] = s0_ref[0]

    h = h_ref[0]
    hn = _rmsnorm(h, gn_ref[...])
    row = lax.broadcasted_iota(jnp.int32, hn.shape, 0)
    prev = jnp.where(row == 0, carry[...], pltpu.roll(hn, 1, 0))
    last = hn[tm - 1:tm, :]
    carry[...] = last
    outs = _rwkv_pre_math(hn, prev, xmix_ref[...], wr_ref[...], wk_ref[...], wv_ref[...], w0_ref[...],
                          w1_ref[...], w2_ref[...], a0_ref[...], a1_ref[...], a2_ref[...], g1_ref[...],
                          g2_ref[...], kk_ref[...], ka_ref[...])
    for buf, o in zip((rb, lwb, kb, vb, ab, bb, gb), outs):
        buf[...] = o

    def chunk(c):
        rows = pl.ds(pl.multiple_of(c * L, L), L)
        ys = _wkv_chunk(rb[rows, :], lwb[rows, :], kb[rows, :], vb[rows, :], ab[rows, :], bb[rows, :], state, L=L)
        for i, y in enumerate(ys):
            yb[rows, i * HEAD:(i + 1) * HEAD] = y

    if tm == L:
        chunk(0)
    else:
        def body(c, carry_):
            chunk(c)
            return carry_
        lax.fori_loop(0, tm // L, body, 0, unroll=True)

    o_ref[0] = _rwkv_post_math(yb[...], rb[...], kb[...], vb[...], gb[...], h, gng_ref[...], gnb_ref[...],
                               rk_ref[...], wo_ref[...])

    @pl.when(t == pl.num_programs(1) - 1)
    def _():
        sh_ref[0] = last
        sout_ref[0] = state[...]


def _rwkv_seq(h, sh0, s0, p, post_p, *, tm, L):
    B, T, D = h.shape
    nh = D // HEAD
    shared = sh0.shape[0] == 1
    sh_map = (lambda b, t: (0, 0, 0)) if shared else (lambda b, t: (b, 0, 0))
    s_map = (lambda b, t: (0, 0, 0, 0)) if shared else (lambda b, t: (b, 0, 0, 0))
    tile = pl.BlockSpec((1, tm, D), lambda b, t: (b, t, 0))
    return pl.pallas_call(
        functools.partial(_rwkv_seq_kernel, tm=tm, L=L),
        grid=(B, T // tm),
        in_specs=([tile, pl.BlockSpec((1, 1, D), sh_map), pl.BlockSpec((1, nh, HEAD, HEAD), s_map)]
                  + _rwkv_weight_specs(D, p) + [_const_spec((1, D))] * 3 + [_const_spec((D, D))]),
        out_specs=[tile, pl.BlockSpec((1, 1, D), lambda b, t: (b, 0, 0)),
                   pl.BlockSpec((1, nh, HEAD, HEAD), lambda b, t: (b, 0, 0, 0))],
        out_shape=[jax.ShapeDtypeStruct((B, T, D), F32), jax.ShapeDtypeStruct((B, 1, D), F32),
                   jax.ShapeDtypeStruct((B, nh, HEAD, HEAD), F32)],
        scratch_shapes=[pltpu.VMEM((1, D), F32), pltpu.VMEM((nh, HEAD, HEAD), F32)]
                       + [pltpu.VMEM((tm, D), F32)] * 8,
        compiler_params=_params(("arbitrary", "arbitrary")),
        name="rwkv_seq",
    )(h, sh0, s0, *_rwkv_weight_args(p), *post_p)


def _wkv_step_kernel(r_ref, lw_ref, k_ref, v_ref, a_ref, b_ref, s_ref, y_ref, sout_ref):
    S = s_ref[0]
    sa = jnp.sum(S * a_ref[...], axis=1, keepdims=True)
    S = S * jnp.exp(lw_ref[...]) + sa * b_ref[...] + v_ref[0] * k_ref[...]
    sout_ref[0] = S
    y_ref[0] = jnp.sum(S * r_ref[...], axis=1, keepdims=True)


def _wkv_step(r, lw, k, v, a, b, s_t):
    R, D = r.shape
    nh = D // HEAD
    rows = lambda x: x.T.reshape(nh, HEAD, R)
    row_spec = pl.BlockSpec((1, HEAD, R), lambda h: (h, 0, 0))
    col_spec = pl.BlockSpec((1, HEAD, 1, R), lambda h: (h, 0, 0, 0))
    st_spec = pl.BlockSpec((1, HEAD, HEAD, R), lambda h: (h, 0, 0, 0))
    y, s_new = pl.pallas_call(
        _wkv_step_kernel,
        grid=(nh,),
        in_specs=[row_spec, row_spec, row_spec, col_spec, row_spec, row_spec, st_spec],
        out_specs=[col_spec, st_spec],
        out_shape=[jax.ShapeDtypeStruct((nh, HEAD, 1, R), F32), jax.ShapeDtypeStruct((nh, HEAD, HEAD, R), F32)],
        compiler_params=_params(("arbitrary",)),
        name="wkv_step",
    )(rows(r), rows(lw), rows(k), v.T.reshape(nh, HEAD, 1, R), rows(a), rows(b), s_t)
    return y.reshape(D, R).T, s_new


def _rwkv_post_kernel(y_ref, r_ref, k_ref, v_ref, g_ref, h_ref, gng_ref, gnb_ref, rk_ref, wo_ref, o_ref):
    o_ref[...] = _rwkv_post_math(y_ref[...], r_ref[...], k_ref[...], v_ref[...], g_ref[...], h_ref[...],
                                 gng_ref[...], gnb_ref[...], rk_ref[...], wo_ref[...])


def _rwkv_post(y, r, k, v, g, h, gng, gnb, rk, wo, *, tm):
    R, D = y.shape
    tile = pl.BlockSpec((tm, D), lambda i: (i, 0))
    return pl.pallas_call(
        _rwkv_post_kernel,
        grid=(R // tm,),
        in_specs=[tile] * 6 + [_const_spec((1, D))] * 3 + [_const_spec((D, D))],
        out_specs=tile,
        out_shape=jax.ShapeDtypeStruct((R, D), F32),
        compiler_params=_params(("arbitrary",)),
        name="rwkv_post",
    )(y, r, k, v, g, h, gng, gnb, rk, wo)


def _row_tile(rows, target):
    tm = min(rows, target)
    while rows % tm:
        tm //= 2
    return tm


def kernel(x_prompt, x_sample, state_conv, state_shift, state_wkv, meta_tokens, norm_mix, norm_mlp, norm_final, conv_w_pw1, conv_b_pw1, conv_w_dw, conv_b_dw, conv_ln_g, conv_ln_b, conv_w_pw2, conv_b_pw2, rwkv_x_mix, rwkv_w_r, rwkv_w_k, rwkv_w_v, rwkv_w_o, rwkv_w0, rwkv_w1, rwkv_w2, rwkv_a0, rwkv_a1, rwkv_a2, rwkv_g1, rwkv_g2, rwkv_k_k, rwkv_k_a, rwkv_r_k, rwkv_gn_g, rwkv_gn_b, w_mlp_in, w_mlp_out):
    B, T, D = x_prompt.shape
    SB = x_sample.shape[0]
    nh = D // HEAD
    depth = norm_mix.shape[0]
    assert depth == 2 and x_sample.shape[1] == 1 and D % MXU_TILE == 0

    row = lambda x: x.reshape(1, -1).astype(F32)
    bf = lambda x: x.astype(BF16)

    conv_p = (row(norm_mix[0]), bf(conv_w_pw1[0]), row(conv_b_pw1[0]), conv_w_dw[0], row(conv_b_dw[0]),
              row(conv_ln_g[0]), row(conv_ln_b[0]), bf(conv_w_pw2[0]), row(conv_b_pw2[0]))
    rwkv_p = dict(gn=row(norm_mix[1]), xmix=rwkv_x_mix[0], wr=bf(rwkv_w_r[0]), wk=bf(rwkv_w_k[0]),
                  wv=bf(rwkv_w_v[0]), w0=row(rwkv_w0[0]), w1=bf(rwkv_w1[0]), w2=bf(rwkv_w2[0]),
                  a0=row(rwkv_a0[0]), a1=bf(rwkv_a1[0]), a2=bf(rwkv_a2[0]), g1=bf(rwkv_g1[0]),
                  g2=bf(rwkv_g2[0]), kk=row(rwkv_k_k[0]), ka=row(rwkv_k_a[0]))
    post_p = (row(rwkv_gn_g[0]), row(rwkv_gn_b[0]), row(rwkv_r_k[0]), bf(rwkv_w_o[0]))
    mlp_p = [(row(norm_mlp[i]), bf(w_mlp_in[i]), bf(w_mlp_out[i])) for i in range(depth)]
    gf = row(norm_final)

    def mlp(x2d, i, final):
        return _mlp(x2d, *mlp_p[i], gf, final=final, tm=_row_tile(x2d.shape[0], 256))

    def seq_trunk(x, buf0, sh0, s0, need_out):
        b, t, _ = x.shape
        h, nb = _conv_seq(x, buf0, *conv_p, *mlp_p[0], tm=_row_tile(t, 512))
        h, sh, s1 = _rwkv_seq(h, sh0, s0, rwkv_p, post_p, tm=_row_tile(t, 512), L=_row_tile(t, 64))
        if not need_out:
            return None, nb, sh, s1
        return mlp(h.reshape(b * t, D), 1, True).reshape(b, t, D), nb, sh, s1

    zero_buf = jnp.zeros((1, BUF_PAD, D), F32)
    zero_sh = jnp.zeros((1, 1, D), F32)
    zero_s = jnp.zeros((1, nh, HEAD, HEAD), F32)
    _, m_buf, m_sh, m_s = seq_trunk(meta_tokens.astype(F32)[None], zero_buf, zero_sh, zero_s, False)

    y_prompt, p_buf, p_sh, p_s = seq_trunk(x_prompt, m_buf, m_sh, m_s, True)
    conv_prompt = p_buf[:, BUF_PAD - CONV_BUF:][None]
    shift_prompt = p_sh.reshape(1, B, D)
    wkv_prompt = p_s[None].astype(state_wkv.dtype)

    xs = x_sample.reshape(SB, D)
    h, hist_t = _conv_step(xs, jnp.swapaxes(state_conv[0], 0, 1), *conv_p)
    conv_sample = jnp.swapaxes(hist_t, 0, 1)[None]
    h = mlp(h, 0, False)
    r, lw, k, v, a, bb, g, sh = _rwkv_pre_step(h, state_shift[0], rwkv_p)
    y, s_t = _wkv_step(r, lw, k, v, a, bb, jnp.transpose(state_wkv[0].astype(F32), (1, 2, 3, 0)))
    s_new = jnp.transpose(s_t, (3, 0, 1, 2))
    h = _rwkv_post(y, r, k, v, g, h, *post_p, tm=SB)
    y_sample = mlp(h, 1, True).reshape(SB, 1, D)
    shift_sample = sh[None]
    wkv_sample = s_new[None].astype(state_wkv.dtype)

    return (y_prompt, y_sample, conv_prompt, shift_prompt, wkv_prompt,
            conv_sample, shift_sample, wkv_sample)
```

```python
import functools
import math

import jax
import jax.numpy as jnp
from jax import lax
from jax.experimental import pallas as pl
from jax.experimental.pallas import tpu as pltpu

F32 = jnp.float32
BF16 = jnp.bfloat16

RMS_EPS = 1e-6
LN_EPS = 1e-5
GN_EPS = 64e-5
L2_EPS = 1e-12
HEAD = 64
CONV_W = 31
CONV_BUF = CONV_W - 1
BUF_PAD = 32
CONV_SUBTILE = 256
MXU_TILE = 256
LANE = 128
SUBLANE = 8
VMEM_LIMIT = 52 * 1024 * 1024


def _mm(a, w):
    return jnp.dot(a.astype(BF16), w, preferred_element_type=F32)


def _dot_nt(a, b):
    return lax.dot_general(a, b, (((1,), (1,)), ((), ())), preferred_element_type=F32)


def _dot_tn(a, b):
    return lax.dot_general(a, b, (((0,), (0,)), ((), ())), preferred_element_type=F32)


def _rmsnorm(x, g):
    ms = jnp.mean(x * x, axis=-1, keepdims=True)
    return x * lax.rsqrt(ms + RMS_EPS) * g


def _head_ones():
    r = lax.broadcasted_iota(jnp.int32, (MXU_TILE, MXU_TILE), 0) // HEAD
    c = lax.broadcasted_iota(jnp.int32, (MXU_TILE, MXU_TILE), 1) // HEAD
    return jnp.where(r == c, 1.0, 0.0).astype(BF16)


def _seg_sum(x, ones_bd):
    hi = x.astype(BF16)
    lo = (x - hi.astype(F32)).astype(BF16)
    outs = []
    for c in range(x.shape[-1] // MXU_TILE):
        sl = slice(c * MXU_TILE, (c + 1) * MXU_TILE)
        s = jnp.dot(hi[:, sl], ones_bd, preferred_element_type=F32)
        s = s + jnp.dot(lo[:, sl], ones_bd, preferred_element_type=F32)
        outs.append(s)
    return jnp.concatenate(outs, axis=-1)


def _const_spec(shape):
    nd = len(shape)
    return pl.BlockSpec(shape, lambda *_: (0,) * nd)


def _weight_spec(shape):
    nd = len(shape)
    return pl.BlockSpec(shape, lambda *_: (0,) * nd, pipeline_mode=pl.Buffered(1))


def _params(sem):
    return pltpu.CompilerParams(dimension_semantics=sem, vmem_limit_bytes=VMEM_LIMIT)


def _conv_pre(x, g, w1, b1):
    d = x.shape[-1]
    u = _mm(_rmsnorm(x, g), w1) + b1
    return u[:, :d] * jax.nn.sigmoid(u[:, d:])


def _conv_post(x, c, lng, lnb, w2, b2):
    mu = jnp.mean(c, axis=-1, keepdims=True)
    cc = c - mu
    var = jnp.mean(cc * cc, axis=-1, keepdims=True)
    cn = cc * lax.rsqrt(var + LN_EPS) * lng + lnb
    return x + _mm(cn * jax.nn.sigmoid(cn), w2) + b2


def _conv_seq_kernel(x_ref, buf0_ref, g_ref, w1_ref, b1_ref, wdw_ref, bdw_ref, lng_ref, lnb_ref,
                     w2_ref, b2_ref, gm_ref, win_ref, wout_ref, h_ref, nb_ref, ubuf, cbuf, shl, *, tm, ts):
    t = pl.program_id(1)
    d = x_ref.shape[-1]

    @pl.when(t == 0)
    def _():
        ubuf[0:BUF_PAD, :] = buf0_ref[0]

    x = x_ref[0]
    ubuf[BUF_PAD:BUF_PAD + tm, :] = _conv_pre(x, g_ref[...], w1_ref[...], b1_ref[...])

    off = BUF_PAD - CONV_BUF
    span = ts + BUF_PAD - SUBLANE

    def conv(i):
        r0 = i * ts
        for l in range(d // LANE):
            ls = slice(l * LANE, (l + 1) * LANE)
            for p in range(1, SUBLANE):
                shl[p - 1] = ubuf[r0 + p:r0 + p + span, ls]
            acc = jnp.broadcast_to(bdw_ref[:, ls], (ts, LANE))
            for j in range(CONV_W):
                q, p = divmod(off + j, SUBLANE)
                if p == 0:
                    src = ubuf[r0 + SUBLANE * q:r0 + SUBLANE * q + ts, ls]
                else:
                    src = shl[p - 1, SUBLANE * q:SUBLANE * q + ts, :]
                acc = acc + wdw_ref[j:j + 1, ls] * src
            cbuf[r0:r0 + ts, ls] = acc

    for i in range(tm // ts):
        conv(i)
    h = _conv_post(x, cbuf[...], lng_ref[...], lnb_ref[...], w2_ref[...], b2_ref[...])
    h_ref[0] = _mlp_math(h, gm_ref[...], win_ref[...], wout_ref[...])
    tail = ubuf[tm:tm + BUF_PAD, :]
    ubuf[0:BUF_PAD, :] = tail

    @pl.when(t == pl.num_programs(1) - 1)
    def _():
        nb_ref[0] = tail


def _conv_seq(x, buf0, g, w1, b1, wdw, bdw, lng, lnb, w2, b2, gm, win, wout, *, tm):
    B, T, D = x.shape
    F = win.shape[1]
    nt = T // tm
    ts = min(tm, CONV_SUBTILE)
    shared = buf0.shape[0] == 1
    buf_map = (lambda b, t: (0, 0, 0)) if shared else (lambda b, t: (b, 0, 0))
    return pl.pallas_call(
        functools.partial(_conv_seq_kernel, tm=tm, ts=ts),
        grid=(B, nt),
        in_specs=[
            pl.BlockSpec((1, tm, D), lambda b, t: (b, t, 0)),
            pl.BlockSpec((1, BUF_PAD, D), buf_map),
            _const_spec((1, D)), _weight_spec((D, 2 * D)), _const_spec((1, 2 * D)),
            _const_spec((CONV_W, D)), _const_spec((1, D)), _const_spec((1, D)), _const_spec((1, D)),
            _weight_spec((D, D)), _const_spec((1, D)),
            _const_spec((1, D)), _weight_spec((D, F)), _weight_spec((F, D)),
        ],
        out_specs=[
            pl.BlockSpec((1, tm, D), lambda b, t: (b, t, 0)),
            pl.BlockSpec((1, BUF_PAD, D), lambda b, t: (b, 0, 0)),
        ],
        out_shape=[jax.ShapeDtypeStruct((B, T, D), F32), jax.ShapeDtypeStruct((B, BUF_PAD, D), F32)],
        scratch_shapes=[pltpu.VMEM((BUF_PAD + tm, D), F32), pltpu.VMEM((tm, D), F32),
                        pltpu.VMEM((SUBLANE - 1, ts + BUF_PAD - SUBLANE, LANE), F32)],
        compiler_params=_params(("arbitrary", "arbitrary")),
        name="conv_seq",
    )(x, buf0, g, w1, b1, wdw, bdw, lng, lnb, w2, b2, gm, win, wout)


def _conv_step_kernel(x_ref, buf_ref, wj_ref, wlast_ref, g_ref, w1_ref, b1_ref, bdw_ref, lng_ref, lnb_ref,
                      w2_ref, b2_ref, h_ref, nb_ref, ubuf, cbuf):
    j = pl.program_id(0)
    n_hist = pl.num_programs(0) - 1

    @pl.when(j == 0)
    def _():
        u = _conv_pre(x_ref[...], g_ref[...], w1_ref[...], b1_ref[...])
        ubuf[...] = u
        cbuf[...] = bdw_ref[...] + wlast_ref[...] * u

    @pl.when(j < n_hist)
    def _():
        cbuf[...] += wj_ref[0] * buf_ref[0]

    @pl.when(jnp.logical_and(j >= 1, j < n_hist))
    def _():
        nb_ref[0] = buf_ref[0]

    @pl.when(j == n_hist)
    def _():
        nb_ref[0] = ubuf[...]
        h_ref[...] = _conv_post(x_ref[...], cbuf[...], lng_ref[...], lnb_ref[...], w2_ref[...], b2_ref[...])


def _conv_step(x, buf_t, g, w1, b1, wdw, bdw, lng, lnb, w2, b2):
    R, D = x.shape
    n_hist = buf_t.shape[0]
    last = n_hist - 1
    return pl.pallas_call(
        _conv_step_kernel,
        grid=(n_hist + 1,),
        in_specs=[
            _const_spec((R, D)),
            pl.BlockSpec((1, R, D), lambda j: (jnp.minimum(j, last), 0, 0)),
            pl.BlockSpec((1, 1, D), lambda j: (jnp.minimum(j, last), 0, 0)),
            _const_spec((1, D)),
            _const_spec((1, D)), _const_spec((D, 2 * D)), _const_spec((1, 2 * D)),
            _const_spec((1, D)), _const_spec((1, D)), _const_spec((1, D)),
            _const_spec((D, D)), _const_spec((1, D)),
        ],
        out_specs=[_const_spec((R, D)), pl.BlockSpec((1, R, D), lambda j: (jnp.maximum(j - 1, 0), 0, 0))],
        out_shape=[jax.ShapeDtypeStruct((R, D), F32), jax.ShapeDtypeStruct((n_hist, R, D), F32)],
        scratch_shapes=[pltpu.VMEM((R, D), F32), pltpu.VMEM((R, D), F32)],
        compiler_params=_params(("arbitrary",)),
        name="conv_step",
    )(x, buf_t, wdw[:n_hist, None, :], wdw[n_hist:], g, w1, b1, bdw, lng, lnb, w2, b2)


def _mlp_math(x, g, win, wout):
    hid = jnp.maximum(_mm(_rmsnorm(x, g), win), 0.0)
    return x + _mm(hid * hid, wout)


def _mlp_kernel(x_ref, g_ref, win_ref, wout_ref, gf_ref, o_ref, *, final):
    y = _mlp_math(x_ref[...], g_ref[...], win_ref[...], wout_ref[...])
    if final:
        y = _rmsnorm(y, gf_ref[...])
    o_ref[...] = y


def _mlp(x, g, win, wout, gf, *, final, tm):
    R, D = x.shape
    F = win.shape[1]
    return pl.pallas_call(
        functools.partial(_mlp_kernel, final=final),
        grid=(R // tm,),
        in_specs=[
            pl.BlockSpec((tm, D), lambda i: (i, 0)),
            _const_spec((1, D)), _weight_spec((D, F)), _weight_spec((F, D)), _const_spec((1, D)),
        ],
        out_specs=pl.BlockSpec((tm, D), lambda i: (i, 0)),
        out_shape=jax.ShapeDtypeStruct((R, D), F32),
        compiler_params=_params(("arbitrary",)),
        name="mlp",
    )(x, g, win, wout, gf)


def _rwkv_pre_math(hn, prev, xmix, wr, wk, wv, w0, w1, w2, a0, a1, a2, g1, g2, kk_w, ka_w):
    xx = prev - hn
    xr = hn + xx * xmix[0:1]
    xw = hn + xx * xmix[1:2]
    xk = hn + xx * xmix[2:3]
    xv = hn + xx * xmix[3:4]
    xa = hn + xx * xmix[4:5]
    xg = hn + xx * xmix[5:6]
    r = _mm(xr, wr)
    k = _mm(xk, wk)
    v = _mm(xv, wv)
    z = w0 + _mm(jnp.tanh(_mm(xw, w1)), w2)
    w_log = -(jnp.maximum(-z, 0.0) + jnp.log(1.0 + jnp.exp(-jnp.abs(z)))) - 0.5
    lw = -jnp.exp(w_log)
    iclr = jax.nn.sigmoid(a0 + _mm(_mm(xa, a1), a2))
    gate = _mm(jax.nn.sigmoid(_mm(xg, g1)), g2)
    kk = k * kk_w
    nrm = jnp.sqrt(_seg_sum(kk * kk, _head_ones()))
    kk = kk / jnp.maximum(nrm, L2_EPS)
    k2 = k * (1.0 + (iclr - 1.0) * ka_w)
    return r, lw, k2, v, -kk, kk * iclr, gate


def _rwkv_weight_specs(D, p):
    return [
        _const_spec((1, D)), _const_spec((6, D)),
        _const_spec((D, D)), _const_spec((D, D)), _const_spec((D, D)),
        _const_spec((1, D)), _const_spec(p["w1"].shape), _const_spec(p["w2"].shape),
        _const_spec((1, D)), _const_spec(p["a1"].shape), _const_spec(p["a2"].shape),
        _const_spec(p["g1"].shape), _const_spec(p["g2"].shape),
        _const_spec((1, D)), _const_spec((1, D)),
    ]


def _rwkv_weight_args(p):
    return (p["gn"], p["xmix"], p["wr"], p["wk"], p["wv"], p["w0"], p["w1"], p["w2"],
            p["a0"], p["a1"], p["a2"], p["g1"], p["g2"], p["kk"], p["ka"])


def _rwkv_pre_step_kernel(h_ref, prev_ref, gn_ref, xmix_ref, wr_ref, wk_ref, wv_ref, w0_ref, w1_ref, w2_ref,
                          a0_ref, a1_ref, a2_ref, g1_ref, g2_ref, kk_ref, ka_ref,
                          r_ref, lw_ref, k_ref, v_ref, a_ref, b_ref, g_ref, sh_ref):
    hn = _rmsnorm(h_ref[...], gn_ref[...])
    outs = _rwkv_pre_math(hn, prev_ref[...], xmix_ref[...], wr_ref[...], wk_ref[...], wv_ref[...], w0_ref[...],
                          w1_ref[...], w2_ref[...], a0_ref[...], a1_ref[...], a2_ref[...], g1_ref[...],
                          g2_ref[...], kk_ref[...], ka_ref[...])
    for o_ref, o in zip((r_ref, lw_ref, k_ref, v_ref, a_ref, b_ref, g_ref), outs):
        o_ref[...] = o
    sh_ref[...] = hn


def _rwkv_pre_step(h, prev, p):
    R, D = h.shape
    full = _const_spec((R, D))
    return pl.pallas_call(
        _rwkv_pre_step_kernel,
        grid=(1,),
        in_specs=[full, full] + _rwkv_weight_specs(D, p),
        out_specs=[full] * 8,
        out_shape=[jax.ShapeDtypeStruct((R, D), F32)] * 8,
        compiler_params=_params(("arbitrary",)),
        name="rwkv_pre_step",
    )(h, prev, *_rwkv_weight_args(p))


def _rwkv_post_math(y, r, k, v, g, h, gng, gnb, rk, wo):
    ones_bd = _head_ones()
    inv_n = 1.0 / HEAD
    mu = _seg_sum(y, ones_bd) * inv_n
    yc = y - mu
    var = _seg_sum(yc * yc, ones_bd) * inv_n
    yn = yc * lax.rsqrt(var + GN_EPS) * gng + gnb
    bonus = _seg_sum(r * k * rk, ones_bd) * v
    return h + _mm((yn + bonus) * g, wo)


def _wkv_chunk(r, lw, k, v, a, b, state, *, L):
    nh = state.shape[0]
    row = lax.broadcasted_iota(jnp.int32, lw.shape, 0)
    cs = lw
    s = 1
    while s < L:
        cs = cs + jnp.where(row >= s, pltpu.roll(cs, s, 0), 0.0)
        s *= 2
    gam = jnp.exp(cs)
    ginv = jnp.exp(-cs)
    g_last = gam[L - 1:L, :]
    rt_f = r * gam
    bt_f = b * ginv
    kt_f = k * ginv
    rt = rt_f.astype(BF16)
    at = (a * jnp.exp(cs - lw)).astype(BF16)
    bt = bt_f.astype(BF16)
    kt = kt_f.astype(BF16)
    bg = (bt_f * g_last).astype(BF16)
    kg = (kt_f * g_last).astype(BF16)
    vb = v.astype(BF16)

    ri = lax.broadcasted_iota(jnp.int32, (L, 2 * L), 0)
    ci = lax.broadcasted_iota(jnp.int32, (L, 2 * L), 1)
    left = ci < L
    cm = jnp.where(left, ci, ci - L)
    strict2 = ri > cm
    incl2 = ri >= cm
    eye_right = jnp.where(jnp.logical_and(ri == cm, jnp.logical_not(left)), 1.0, 0.0)
    zeros_l = jnp.zeros((L, HEAD), BF16)
    heads = range(nh)
    hs = lambda x, h: x[:, h * HEAD:(h + 1) * HEAD]
    dot = lambda x, y: jnp.dot(x, y, preferred_element_type=F32)
    cat0 = lambda x, y: jnp.concatenate([x, y], axis=0)
    cat1 = lambda x, y: jnp.concatenate([x, y], axis=1)

    bk_t = jnp.transpose(cat0(bt_f, kt_f)).astype(BF16)
    P = [dot(cat0(hs(at, h), hs(rt, h)), bk_t[h * HEAD:(h + 1) * HEAD, :]) for h in heads]
    m_a = [jnp.where(strict2, p[:L], 0.0) for p in P]
    m_r = [jnp.where(incl2, p[L:], 0.0).astype(BF16) for p in P]
    makv = [dot(m_a[h][:, L:].astype(BF16), hs(vb, h)).astype(BF16) for h in heads]
    W = [jnp.where(left, m, 0.0) + eye_right for m in m_a]
    for _ in range(int(math.log2(L))):
        Z = [dot(W[h][:, :L].astype(BF16), W[h].astype(BF16)) for h in heads]
        W = [jnp.where(left, Z[h], W[h] + Z[h]) for h in heads]
    tfin = [w[:, L:].astype(BF16) for w in W]
    TX = [dot(tfin[h], cat1(hs(at, h), makv[h])).astype(BF16) for h in heads]
    XV = [cat0(TX[h], cat1(zeros_l, hs(vb, h))) for h in heads]
    RY = [dot(m_r[h], XV[h]) for h in heads]
    rbar = [(hs(rt_f, h) + RY[h][:, :HEAD]).astype(BF16) for h in heads]
    GH = [_dot_tn(XV[h], cat0(hs(bg, h), hs(kg, h))) for h in heads]
    ys = []
    for h in heads:
        S = state[h]
        Sb = S.astype(BF16)
        ys.append(_dot_nt(rbar[h], Sb) + RY[h][:, HEAD:])
        state[h] = S * hs(g_last, h) + dot(Sb, GH[h][:HEAD].astype(BF16)) + GH[h][HEAD:]
    return ys


def _rwkv_seq_kernel(h_ref, sh0_ref, s0_ref, gn_ref, xmix_ref, wr_ref, wk_ref, wv_ref, w0_ref, w1_ref, w2_ref,
                     a0_ref, a1_ref, a2_ref, g1_ref, g2_ref, kk_ref, ka_ref, gng_ref, gnb_ref, rk_ref, wo_ref,
                     o_ref, sh_ref, sout_ref, carry, state, rb, lwb, kb, vb, ab, bb, gb, yb, *, tm, L):
    t = pl.program_id(1)

    @pl.when(t == 0)
    def _():
        carry[...] = sh0_ref[0]
        state[...] = s0_ref[0]

    h = h_ref[0]
    hn = _rmsnorm(h, gn_ref[...])
    row = lax.broadcasted_iota(jnp.int32, hn.shape, 0)
    prev = jnp.where(row == 0, carry[...], pltpu.roll(hn, 1, 0))
    last = hn[tm - 1:tm, :]
    carry[...] = last
    outs = _rwkv_pre_math(hn, prev, xmix_ref[...], wr_ref[...], wk_ref[...], wv_ref[...], w0_ref[...],
                          w1_ref[...], w2_ref[...], a0_ref[...], a1_ref[...], a2_ref[...], g1_ref[...],
                          g2_ref[...], kk_ref[...], ka_ref[...])
    for buf, o in zip((rb, lwb, kb, vb, ab, bb, gb), outs):
        buf[...] = o

    def chunk(c):
        rows = pl.ds(pl.multiple_of(c * L, L), L)
        ys = _wkv_chunk(rb[rows, :], lwb[rows, :], kb[rows, :], vb[rows, :], ab[rows, :], bb[rows, :], state, L=L)
        for i, y in enumerate(ys):
            yb[rows, i * HEAD:(i + 1) * HEAD] = y

    if tm == L:
        chunk(0)
    else:
        def body(c, carry_):
            chunk(c)
            return carry_
        lax.fori_loop(0, tm // L, body, 0, unroll=True)

    o_ref[0] = _rwkv_post_math(yb[...], rb[...], kb[...], vb[...], gb[...], h, gng_ref[...], gnb_ref[...],
                               rk_ref[...], wo_ref[...])

    @pl.when(t == pl.num_programs(1) - 1)
    def _():
        sh_ref[0] = last
        sout_ref[0] = state[...]


def _rwkv_seq(h, sh0, s0, p, post_p, *, tm, L):
    B, T, D = h.shape
    nh = D // HEAD
    shared = sh0.shape[0] == 1
    sh_map = (lambda b, t: (0, 0, 0)) if shared else (lambda b, t: (b, 0, 0))
    s_map = (lambda b, t: (0, 0, 0, 0)) if shared else (lambda b, t: (b, 0, 0, 0))
    tile = pl.BlockSpec((1, tm, D), lambda b, t: (b, t, 0))
    return pl.pallas_call(
        functools.partial(_rwkv_seq_kernel, tm=tm, L=L),
        grid=(B, T // tm),
        in_specs=([tile, pl.BlockSpec((1, 1, D), sh_map), pl.BlockSpec((1, nh, HEAD, HEAD), s_map)]
                  + _rwkv_weight_specs(D, p) + [_const_spec((1, D))] * 3 + [_const_spec((D, D))]),
        out_specs=[tile, pl.BlockSpec((1, 1, D), lambda b, t: (b, 0, 0)),
                   pl.BlockSpec((1, nh, HEAD, HEAD), lambda b, t: (b, 0, 0, 0))],
        out_shape=[jax.ShapeDtypeStruct((B, T, D), F32), jax.ShapeDtypeStruct((B, 1, D), F32),
                   jax.ShapeDtypeStruct((B, nh, HEAD, HEAD), F32)],
        scratch_shapes=[pltpu.VMEM((1, D), F32), pltpu.VMEM((nh, HEAD, HEAD), F32)]
                       + [pltpu.VMEM((tm, D), F32)] * 8,
        compiler_params=_params(("arbitrary", "arbitrary")),
        name="rwkv_seq",
    )(h, sh0, s0, *_rwkv_weight_args(p), *post_p)


def _wkv_step_kernel(r_ref, lw_ref, k_ref, v_ref, a_ref, b_ref, s_ref, y_ref, sout_ref):
    S = s_ref[0]
    sa = jnp.sum(S * a_ref[...], axis=1, keepdims=True)
    S = S * jnp.exp(lw_ref[...]) + sa * b_ref[...] + v_ref[0] * k_ref[...]
    sout_ref[0] = S
    y_ref[0] = jnp.sum(S * r_ref[...], axis=1, keepdims=True)


def _wkv_step(r, lw, k, v, a, b, s_t):
    R, D = r.shape
    nh = D // HEAD
    rows = lambda x: x.T.reshape(nh, HEAD, R)
    row_spec = pl.BlockSpec((1, HEAD, R), lambda h: (h, 0, 0))
    col_spec = pl.BlockSpec((1, HEAD, 1, R), lambda h: (h, 0, 0, 0))
    st_spec = pl.BlockSpec((1, HEAD, HEAD, R), lambda h: (h, 0, 0, 0))
    y, s_new = pl.pallas_call(
        _wkv_step_kernel,
        grid=(nh,),
        in_specs=[row_spec, row_spec, row_spec, col_spec, row_spec, row_spec, st_spec],
        out_specs=[col_spec, st_spec],
        out_shape=[jax.ShapeDtypeStruct((nh, HEAD, 1, R), F32), jax.ShapeDtypeStruct((nh, HEAD, HEAD, R), F32)],
        compiler_params=_params(("arbitrary",)),
        name="wkv_step",
    )(rows(r), rows(lw), rows(k), v.T.reshape(nh, HEAD, 1, R), rows(a), rows(b), s_t)
    return y.reshape(D, R).T, s_new


def _rwkv_post_kernel(y_ref, r_ref, k_ref, v_ref, g_ref, h_ref, gng_ref, gnb_ref, rk_ref, wo_ref, o_ref):
    o_ref[...] = _rwkv_post_math(y_ref[...], r_ref[...], k_ref[...], v_ref[...], g_ref[...], h_ref[...],
                                 gng_ref[...], gnb_ref[...], rk_ref[...], wo_ref[...])


def _rwkv_post(y, r, k, v, g, h, gng, gnb, rk, wo, *, tm):
    R, D = y.shape
    tile = pl.BlockSpec((tm, D), lambda i: (i, 0))
    return pl.pallas_call(
        _rwkv_post_kernel,
        grid=(R // tm,),
        in_specs=[tile] * 6 + [_const_spec((1, D))] * 3 + [_const_spec((D, D))],
        out_specs=tile,
        out_shape=jax.ShapeDtypeStruct((R, D), F32),
        compiler_params=_params(("arbitrary",)),
        name="rwkv_post",
    )(y, r, k, v, g, h, gng, gnb, rk, wo)


def _row_tile(rows, target):
    tm = min(rows, target)
    while rows % tm:
        tm //= 2
    return tm


def kernel(x_prompt, x_sample, state_conv, state_shift, state_wkv, meta_tokens, norm_mix, norm_mlp, norm_final, conv_w_pw1, conv_b_pw1, conv_w_dw, conv_b_dw, conv_ln_g, conv_ln_b, conv_w_pw2, conv_b_pw2, rwkv_x_mix, rwkv_w_r, rwkv_w_k, rwkv_w_v, rwkv_w_o, rwkv_w0, rwkv_w1, rwkv_w2, rwkv_a0, rwkv_a1, rwkv_a2, rwkv_g1, rwkv_g2, rwkv_k_k, rwkv_k_a, rwkv_r_k, rwkv_gn_g, rwkv_gn_b, w_mlp_in, w_mlp_out):
    B, T, D = x_prompt.shape
    SB = x_sample.shape[0]
    nh = D // HEAD
    depth = norm_mix.shape[0]
    assert depth == 2 and x_sample.shape[1] == 1 and D % MXU_TILE == 0

    row = lambda x: x.reshape(1, -1).astype(F32)
    bf = lambda x: x.astype(BF16)

    conv_p = (row(norm_mix[0]), bf(conv_w_pw1[0]), row(conv_b_pw1[0]), conv_w_dw[0], row(conv_b_dw[0]),
              row(conv_ln_g[0]), row(conv_ln_b[0]), bf(conv_w_pw2[0]), row(conv_b_pw2[0]))
    rwkv_p = dict(gn=row(norm_mix[1]), xmix=rwkv_x_mix[0], wr=bf(rwkv_w_r[0]), wk=bf(rwkv_w_k[0]),
                  wv=bf(rwkv_w_v[0]), w0=row(rwkv_w0[0]), w1=bf(rwkv_w1[0]), w2=bf(rwkv_w2[0]),
                  a0=row(rwkv_a0[0]), a1=bf(rwkv_a1[0]), a2=bf(rwkv_a2[0]), g1=bf(rwkv_g1[0]),
                  g2=bf(rwkv_g2[0]), kk=row(rwkv_k_k[0]), ka=row(rwkv_k_a[0]))
    post_p = (row(rwkv_gn_g[0]), row(rwkv_gn_b[0]), row(rwkv_r_k[0]), bf(rwkv_w_o[0]))
    mlp_p = [(row(norm_mlp[i]), bf(w_mlp_in[i]), bf(w_mlp_out[i])) for i in range(depth)]
    gf = row(norm_final)

    def mlp(x2d, i, final):
        return _mlp(x2d, *mlp_p[i], gf, final=final, tm=_row_tile(x2d.shape[0], 512))

    def seq_trunk(x, buf0, sh0, s0, need_out):
        b, t, _ = x.shape
        h, nb = _conv_seq(x, buf0, *conv_p, *mlp_p[0], tm=_row_tile(t, 512))
        h, sh, s1 = _rwkv_seq(h, sh0, s0, rwkv_p, post_p, tm=_row_tile(t, 512), L=_row_tile(t, 64))
        if not need_out:
            return None, nb, sh, s1
        return mlp(h.reshape(b * t, D), 1, True).reshape(b, t, D), nb, sh, s1

    zero_buf = jnp.zeros((1, BUF_PAD, D), F32)
    zero_sh = jnp.zeros((1, 1, D), F32)
    zero_s = jnp.zeros((1, nh, HEAD, HEAD), F32)
    _, m_buf, m_sh, m_s = seq_trunk(meta_tokens.astype(F32)[None], zero_buf, zero_sh, zero_s, False)

    y_prompt, p_buf, p_sh, p_s = seq_trunk(x_prompt, m_buf, m_sh, m_s, True)
    conv_prompt = p_buf[:, BUF_PAD - CONV_BUF:][None]
    shift_prompt = p_sh.reshape(1, B, D)
    wkv_prompt = p_s[None].astype(state_wkv.dtype)

    xs = x_sample.reshape(SB, D)
    h, hist_t = _conv_step(xs, jnp.swapaxes(state_conv[0], 0, 1), *conv_p)
    conv_sample = jnp.swapaxes(hist_t, 0, 1)[None]
    h = mlp(h, 0, False)
    r, lw, k, v, a, bb, g, sh = _rwkv_pre_step(h, state_shift[0], rwkv_p)
    y, s_t = _wkv_step(r, lw, k, v, a, bb, jnp.transpose(state_wkv[0].astype(F32), (1, 2, 3, 0)))
    s_new = jnp.transpose(s_t, (3, 0, 1, 2))
    h = _rwkv_post(y, r, k, v, g, h, *post_p, tm=SB)
    y_sample = mlp(h, 1, True).reshape(SB, 1, D)
    shift_sample = sh[None]
    wkv_sample = s_new[None].astype(state_wkv.dtype)

    return (y_prompt, y_sample, conv_prompt, shift_prompt, wkv_prompt,
            conv_sample, shift_sample, wkv_sample)
```

```python
import functools
import math

import jax
import jax.numpy as jnp
from jax import lax
from jax.experimental import pallas as pl
from jax.experimental.pallas import tpu as pltpu

F32 = jnp.float32
BF16 = jnp.bfloat16

RMS_EPS = 1e-6
LN_EPS = 1e-5
GN_EPS = 64e-5
L2_EPS = 1e-12
HEAD = 64
CONV_W = 31
CONV_BUF = CONV_W - 1
BUF_PAD = 32
CONV_SUBTILE = 256
MXU_TILE = 256
LANE = 128
SUBLANE = 8
VMEM_LIMIT = 52 * 1024 * 1024


def _mm(a, w):
    return jnp.dot(a.astype(BF16), w, preferred_element_type=F32)


def _dot_nt(a, b):
    return lax.dot_general(a, b, (((1,), (1,)), ((), ())), preferred_element_type=F32)


def _dot_tn(a, b):
    return lax.dot_general(a, b, (((0,), (0,)), ((), ())), preferred_element_type=F32)


def _rmsnorm(x, g):
    ms = jnp.mean(x * x, axis=-1, keepdims=True)
    return x * lax.rsqrt(ms + RMS_EPS) * g


def _head_ones():
    r = lax.broadcasted_iota(jnp.int32, (MXU_TILE, MXU_TILE), 0) // HEAD
    c = lax.broadcasted_iota(jnp.int32, (MXU_TILE, MXU_TILE), 1) // HEAD
    return jnp.where(r == c, 1.0, 0.0).astype(BF16)


def _seg_sum(x, ones_bd):
    hi = x.astype(BF16)
    lo = (x - hi.astype(F32)).astype(BF16)
    outs = []
    for c in range(x.shape[-1] // MXU_TILE):
        sl = slice(c * MXU_TILE, (c + 1) * MXU_TILE)
        s = jnp.dot(hi[:, sl], ones_bd, preferred_element_type=F32)
        s = s + jnp.dot(lo[:, sl], ones_bd, preferred_element_type=F32)
        outs.append(s)
    return jnp.concatenate(outs, axis=-1)


def _const_spec(shape):
    nd = len(shape)
    return pl.BlockSpec(shape, lambda *_: (0,) * nd)


def _weight_spec(shape):
    nd = len(shape)
    return pl.BlockSpec(shape, lambda *_: (0,) * nd, pipeline_mode=pl.Buffered(1))


def _params(sem):
    return pltpu.CompilerParams(dimension_semantics=sem, vmem_limit_bytes=VMEM_LIMIT)


def _conv_pre(x, g, w1, b1):
    d = x.shape[-1]
    u = _mm(_rmsnorm(x, g), w1) + b1
    return u[:, :d] * jax.nn.sigmoid(u[:, d:])


def _conv_post(x, c, lng, lnb, w2, b2):
    mu = jnp.mean(c, axis=-1, keepdims=True)
    cc = c - mu
    var = jnp.mean(cc * cc, axis=-1, keepdims=True)
    cn = cc * lax.rsqrt(var + LN_EPS) * lng + lnb
    return x + _mm(cn * jax.nn.sigmoid(cn), w2) + b2


def _conv_seq_kernel(x_ref, buf0_ref, g_ref, w1_ref, b1_ref, wdw_ref, bdw_ref, lng_ref, lnb_ref,
                     w2_ref, b2_ref, gm_ref, win_ref, wout_ref, h_ref, nb_ref, ubuf, cbuf, shl, *, tm, ts):
    t = pl.program_id(1)
    d = x_ref.shape[-1]

    @pl.when(t == 0)
    def _():
        ubuf[0:BUF_PAD, :] = buf0_ref[0]

    x = x_ref[0]
    ubuf[BUF_PAD:BUF_PAD + tm, :] = _conv_pre(x, g_ref[...], w1_ref[...], b1_ref[...])

    off = BUF_PAD - CONV_BUF
    span = ts + BUF_PAD - SUBLANE

    def conv(i):
        r0 = i * ts
        for l in range(d // LANE):
            ls = slice(l * LANE, (l + 1) * LANE)
            for p in range(1, SUBLANE):
                shl[p - 1] = ubuf[r0 + p:r0 + p + span, ls]
            acc = jnp.broadcast_to(bdw_ref[:, ls], (ts, LANE))
            for j in range(CONV_W):
                q, p = divmod(off + j, SUBLANE)
                if p == 0:
                    src = ubuf[r0 + SUBLANE * q:r0 + SUBLANE * q + ts, ls]
                else:
                    src = shl[p - 1, SUBLANE * q:SUBLANE * q + ts, :]
                acc = acc + wdw_ref[j:j + 1, ls] * src
            cbuf[r0:r0 + ts, ls] = acc

    for i in range(tm // ts):
        conv(i)
    h = _conv_post(x, cbuf[...], lng_ref[...], lnb_ref[...], w2_ref[...], b2_ref[...])
    h_ref[0] = _mlp_math(h, gm_ref[...], win_ref[...], wout_ref[...])
    tail = ubuf[tm:tm + BUF_PAD, :]
    ubuf[0:BUF_PAD, :] = tail

    @pl.when(t == pl.num_programs(1) - 1)
    def _():
        nb_ref[0] = tail


def _conv_seq(x, buf0, g, w1, b1, wdw, bdw, lng, lnb, w2, b2, gm, win, wout, *, tm):
    B, T, D = x.shape
    F = win.shape[1]
    nt = T // tm
    ts = min(tm, CONV_SUBTILE)
    shared = buf0.shape[0] == 1
    buf_map = (lambda b, t: (0, 0, 0)) if shared else (lambda b, t: (b, 0, 0))
    return pl.pallas_call(
        functools.partial(_conv_seq_kernel, tm=tm, ts=ts),
        grid=(B, nt),
        in_specs=[
            pl.BlockSpec((1, tm, D), lambda b, t: (b, t, 0)),
            pl.BlockSpec((1, BUF_PAD, D), buf_map),
            _const_spec((1, D)), _weight_spec((D, 2 * D)), _const_spec((1, 2 * D)),
            _const_spec((CONV_W, D)), _const_spec((1, D)), _const_spec((1, D)), _const_spec((1, D)),
            _weight_spec((D, D)), _const_spec((1, D)),
            _const_spec((1, D)), _weight_spec((D, F)), _weight_spec((F, D)),
        ],
        out_specs=[
            pl.BlockSpec((1, tm, D), lambda b, t: (b, t, 0)),
            pl.BlockSpec((1, BUF_PAD, D), lambda b, t: (b, 0, 0)),
        ],
        out_shape=[jax.ShapeDtypeStruct((B, T, D), F32), jax.ShapeDtypeStruct((B, BUF_PAD, D), F32)],
        scratch_shapes=[pltpu.VMEM((BUF_PAD + tm, D), F32), pltpu.VMEM((tm, D), F32),
                        pltpu.VMEM((SUBLANE - 1, ts + BUF_PAD - SUBLANE, LANE), F32)],
        compiler_params=_params(("arbitrary", "arbitrary")),
        name="conv_seq",
    )(x, buf0, g, w1, b1, wdw, bdw, lng, lnb, w2, b2, gm, win, wout)


def _conv_step_kernel(x_ref, buf_ref, wj_ref, wlast_ref, g_ref, w1_ref, b1_ref, bdw_ref, lng_ref, lnb_ref,
                      w2_ref, b2_ref, h_ref, nb_ref, ubuf, cbuf, keep, *, rb):
    i = pl.program_id(0)
    n_blk = pl.num_programs(0) - 1

    @pl.when(i == 0)
    def _():
        u = _conv_pre(x_ref[...], g_ref[...], w1_ref[...], b1_ref[...])
        ubuf[...] = u
        cbuf[...] = bdw_ref[...] + wlast_ref[...] * u

    @pl.when(i < n_blk)
    def _():
        acc = cbuf[...]
        for r in range(rb):
            acc = acc + wj_ref[r] * buf_ref[r]
        cbuf[...] = acc

    @pl.when(jnp.logical_and(i >= 1, i < n_blk))
    def _():
        nb_ref[0:rb - 1] = keep[...]
        nb_ref[rb - 1] = buf_ref[0]

    @pl.when(i < n_blk)
    def _():
        keep[...] = buf_ref[1:rb]

    @pl.when(i == n_blk)
    def _():
        nb_ref[0:rb - 1] = keep[...]
        nb_ref[rb - 1] = ubuf[...]
        h_ref[...] = _conv_post(x_ref[...], cbuf[...], lng_ref[...], lnb_ref[...], w2_ref[...], b2_ref[...])


def _conv_step(x, buf_t, g, w1, b1, wdw, bdw, lng, lnb, w2, b2, *, rb):
    R, D = x.shape
    n_hist = buf_t.shape[0]
    assert n_hist % rb == 0 and rb >= 2
    last = n_hist // rb - 1
    return pl.pallas_call(
        functools.partial(_conv_step_kernel, rb=rb),
        grid=(n_hist // rb + 1,),
        in_specs=[
            _const_spec((R, D)),
            pl.BlockSpec((rb, R, D), lambda i: (jnp.minimum(i, last), 0, 0)),
            pl.BlockSpec((rb, 1, D), lambda i: (jnp.minimum(i, last), 0, 0)),
            _const_spec((1, D)),
            _const_spec((1, D)), _const_spec((D, 2 * D)), _const_spec((1, 2 * D)),
            _const_spec((1, D)), _const_spec((1, D)), _const_spec((1, D)),
            _const_spec((D, D)), _const_spec((1, D)),
        ],
        out_specs=[_const_spec((R, D)), pl.BlockSpec((rb, R, D), lambda i: (jnp.maximum(i - 1, 0), 0, 0))],
        out_shape=[jax.ShapeDtypeStruct((R, D), F32), jax.ShapeDtypeStruct((n_hist, R, D), F32)],
        scratch_shapes=[pltpu.VMEM((R, D), F32), pltpu.VMEM((R, D), F32), pltpu.VMEM((rb - 1, R, D), F32)],
        compiler_params=_params(("arbitrary",)),
        name="conv_step",
    )(x, buf_t, wdw[:n_hist, None, :], wdw[n_hist:], g, w1, b1, bdw, lng, lnb, w2, b2)


def _mlp_math(x, g, win, wout):
    hid = jnp.maximum(_mm(_rmsnorm(x, g), win), 0.0)
    return x + _mm(hid * hid, wout)


def _mlp_kernel(x_ref, g_ref, win_ref, wout_ref, gf_ref, o_ref, *, final):
    y = _mlp_math(x_ref[...], g_ref[...], win_ref[...], wout_ref[...])
    if final:
        y = _rmsnorm(y, gf_ref[...])
    o_ref[...] = y


def _mlp(x, g, win, wout, gf, *, final, tm):
    R, D = x.shape
    F = win.shape[1]
    return pl.pallas_call(
        functools.partial(_mlp_kernel, final=final),
        grid=(R // tm,),
        in_specs=[
            pl.BlockSpec((tm, D), lambda i: (i, 0)),
            _const_spec((1, D)), _weight_spec((D, F)), _weight_spec((F, D)), _const_spec((1, D)),
        ],
        out_specs=pl.BlockSpec((tm, D), lambda i: (i, 0)),
        out_shape=jax.ShapeDtypeStruct((R, D), F32),
        compiler_params=_params(("arbitrary",)),
        name="mlp",
    )(x, g, win, wout, gf)


def _rwkv_pre_math(hn, prev, xmix, wr, wk, wv, w0, w1, w2, a0, a1, a2, g1, g2, kk_w, ka_w):
    xx = prev - hn
    xr = hn + xx * xmix[0:1]
    xw = hn + xx * xmix[1:2]
    xk = hn + xx * xmix[2:3]
    xv = hn + xx * xmix[3:4]
    xa = hn + xx * xmix[4:5]
    xg = hn + xx * xmix[5:6]
    r = _mm(xr, wr)
    k = _mm(xk, wk)
    v = _mm(xv, wv)
    z = w0 + _mm(jnp.tanh(_mm(xw, w1)), w2)
    w_log = -(jnp.maximum(-z, 0.0) + jnp.log(1.0 + jnp.exp(-jnp.abs(z)))) - 0.5
    lw = -jnp.exp(w_log)
    iclr = jax.nn.sigmoid(a0 + _mm(_mm(xa, a1), a2))
    gate = _mm(jax.nn.sigmoid(_mm(xg, g1)), g2)
    kk = k * kk_w
    nrm = jnp.sqrt(_seg_sum(kk * kk, _head_ones()))
    kk = kk / jnp.maximum(nrm, L2_EPS)
    k2 = k * (1.0 + (iclr - 1.0) * ka_w)
    return r, lw, k2, v, -kk, kk * iclr, gate


def _rwkv_weight_specs(D, p):
    return [
        _const_spec((1, D)), _const_spec((6, D)),
        _const_spec((D, D)), _const_spec((D, D)), _const_spec((D, D)),
        _const_spec((1, D)), _const_spec(p["w1"].shape), _const_spec(p["w2"].shape),
        _const_spec((1, D)), _const_spec(p["a1"].shape), _const_spec(p["a2"].shape),
        _const_spec(p["g1"].shape), _const_spec(p["g2"].shape),
        _const_spec((1, D)), _const_spec((1, D)),
    ]


def _rwkv_weight_args(p):
    return (p["gn"], p["xmix"], p["wr"], p["wk"], p["wv"], p["w0"], p["w1"], p["w2"],
            p["a0"], p["a1"], p["a2"], p["g1"], p["g2"], p["kk"], p["ka"])


def _rwkv_pre_step_kernel(h_ref, prev_ref, gn_ref, xmix_ref, wr_ref, wk_ref, wv_ref, w0_ref, w1_ref, w2_ref,
                          a0_ref, a1_ref, a2_ref, g1_ref, g2_ref, kk_ref, ka_ref,
                          r_ref, lw_ref, k_ref, v_ref, a_ref, b_ref, g_ref, sh_ref):
    hn = _rmsnorm(h_ref[...], gn_ref[...])
    outs = _rwkv_pre_math(hn, prev_ref[...], xmix_ref[...], wr_ref[...], wk_ref[...], wv_ref[...], w0_ref[...],
                          w1_ref[...], w2_ref[...], a0_ref[...], a1_ref[...], a2_ref[...], g1_ref[...],
                          g2_ref[...], kk_ref[...], ka_ref[...])
    for o_ref, o in zip((r_ref, lw_ref, k_ref, v_ref, a_ref, b_ref, g_ref), outs):
        o_ref[...] = o
    sh_ref[...] = hn


def _rwkv_pre_step(h, prev, p):
    R, D = h.shape
    full = _const_spec((R, D))
    return pl.pallas_call(
        _rwkv_pre_step_kernel,
        grid=(1,),
        in_specs=[full, full] + _rwkv_weight_specs(D, p),
        out_specs=[full] * 8,
        out_shape=[jax.ShapeDtypeStruct((R, D), F32)] * 8,
        compiler_params=_params(("arbitrary",)),
        name="rwkv_pre_step",
    )(h, prev, *_rwkv_weight_args(p))


def _rwkv_post_math(y, r, k, v, g, h, gng, gnb, rk, wo):
    ones_bd = _head_ones()
    inv_n = 1.0 / HEAD
    mu = _seg_sum(y, ones_bd) * inv_n
    yc = y - mu
    var = _seg_sum(yc * yc, ones_bd) * inv_n
    yn = yc * lax.rsqrt(var + GN_EPS) * gng + gnb
    bonus = _seg_sum(r * k * rk, ones_bd) * v
    return h + _mm((yn + bonus) * g, wo)


def _wkv_chunk(r, lw, k, v, a, b, state, *, L):
    nh = state.shape[0]
    row = lax.broadcasted_iota(jnp.int32, lw.shape, 0)
    cs = lw
    s = 1
    while s < L:
        cs = cs + jnp.where(row >= s, pltpu.roll(cs, s, 0), 0.0)
        s *= 2
    gam = jnp.exp(cs)
    ginv = jnp.exp(-cs)
    g_last = gam[L - 1:L, :]
    rt_f = r * gam
    bt_f = b * ginv
    kt_f = k * ginv
    rt = rt_f.astype(BF16)
    at = (a * jnp.exp(cs - lw)).astype(BF16)
    bt = bt_f.astype(BF16)
    kt = kt_f.astype(BF16)
    bg = (bt_f * g_last).astype(BF16)
    kg = (kt_f * g_last).astype(BF16)
    vb = v.astype(BF16)

    ri = lax.broadcasted_iota(jnp.int32, (L, 2 * L), 0)
    ci = lax.broadcasted_iota(jnp.int32, (L, 2 * L), 1)
    left = ci < L
    cm = jnp.where(left, ci, ci - L)
    strict2 = ri > cm
    incl2 = ri >= cm
    eye_right = jnp.where(jnp.logical_and(ri == cm, jnp.logical_not(left)), 1.0, 0.0)
    zeros_l = jnp.zeros((L, HEAD), BF16)
    heads = range(nh)
    hs = lambda x, h: x[:, h * HEAD:(h + 1) * HEAD]
    dot = lambda x, y: jnp.dot(x, y, preferred_element_type=F32)
    cat0 = lambda x, y: jnp.concatenate([x, y], axis=0)
    cat1 = lambda x, y: jnp.concatenate([x, y], axis=1)

    bk_t = jnp.transpose(cat0(bt_f, kt_f)).astype(BF16)
    P = [dot(cat0(hs(at, h), hs(rt, h)), bk_t[h * HEAD:(h + 1) * HEAD, :]) for h in heads]
    m_a = [jnp.where(strict2, p[:L], 0.0) for p in P]
    m_r = [jnp.where(incl2, p[L:], 0.0).astype(BF16) for p in P]
    makv = [dot(m_a[h][:, L:].astype(BF16), hs(vb, h)).astype(BF16) for h in heads]
    W = [jnp.where(left, m, 0.0) + eye_right for m in m_a]
    for _ in range(int(math.log2(L))):
        Z = [dot(W[h][:, :L].astype(BF16), W[h].astype(BF16)) for h in heads]
        W = [jnp.where(left, Z[h], W[h] + Z[h]) for h in heads]
    tfin = [w[:, L:].astype(BF16) for w in W]
    TX = [dot(tfin[h], cat1(hs(at, h), makv[h])).astype(BF16) for h in heads]
    XV = [cat0(TX[h], cat1(zeros_l, hs(vb, h))) for h in heads]
    RY = [dot(m_r[h], XV[h]) for h in heads]
    rbar = [(hs(rt_f, h) + RY[h][:, :HEAD]).astype(BF16) for h in heads]
    GH = [_dot_tn(XV[h], cat0(hs(bg, h), hs(kg, h))) for h in heads]
    ys = []
    for h in heads:
        S = state[h]
        Sb = S.astype(BF16)
        ys.append(_dot_nt(rbar[h], Sb) + RY[h][:, HEAD:])
        state[h] = S * hs(g_last, h) + dot(Sb, GH[h][:HEAD].astype(BF16)) + GH[h][HEAD:]
    return ys


def _rwkv_seq_kernel(h_ref, sh0_ref, s0_ref, gn_ref, xmix_ref, wr_ref, wk_ref, wv_ref, w0_ref, w1_ref, w2_ref,
                     a0_ref, a1_ref, a2_ref, g1_ref, g2_ref, kk_ref, ka_ref, gng_ref, gnb_ref, rk_ref, wo_ref,
                     o_ref, sh_ref, sout_ref, carry, state, rb, lwb, kb, vb, ab, bb, gb, yb, *, tm, L):
    t = pl.program_id(1)

    @pl.when(t == 0)
    def _():
        carry[...] = sh0_ref[0]
        state[...] = s0_ref[0]

    h = h_ref[0]
    hn = _rmsnorm(h, gn_ref[...])
    row = lax.broadcasted_iota(jnp.int32, hn.shape, 0)
    prev = jnp.where(row == 0, carry[...], pltpu.roll(hn, 1, 0))
    last = hn[tm - 1:tm, :]
    carry[...] = last
    outs = _rwkv_pre_math(hn, prev, xmix_ref[...], wr_ref[...], wk_ref[...], wv_ref[...], w0_ref[...],
                          w1_ref[...], w2_ref[...], a0_ref[...], a1_ref[...], a2_ref[...], g1_ref[...],
                          g2_ref[...], kk_ref[...], ka_ref[...])
    for buf, o in zip((rb, lwb, kb, vb, ab, bb, gb), outs):
        buf[...] = o

    def chunk(c):
        rows = pl.ds(pl.multiple_of(c * L, L), L)
        ys = _wkv_chunk(rb[rows, :], lwb[rows, :], kb[rows, :], vb[rows, :], ab[rows, :], bb[rows, :], state, L=L)
        for i, y in enumerate(ys):
            yb[rows, i * HEAD:(i + 1) * HEAD] = y

    if tm == L:
        chunk(0)
    else:
        def body(c, carry_):
            chunk(c)
            return carry_
        lax.fori_loop(0, tm // L, body, 0, unroll=True)

    o_ref[0] = _rwkv_post_math(yb[...], rb[...], kb[...], vb[...], gb[...], h, gng_ref[...], gnb_ref[...],
                               rk_ref[...], wo_ref[...])

    @pl.when(t == pl.num_programs(1) - 1)
    def _():
        sh_ref[0] = last
        sout_ref[0] = state[...]


def _rwkv_seq(h, sh0, s0, p, post_p, *, tm, L):
    B, T, D = h.shape
    nh = D // HEAD
    shared = sh0.shape[0] == 1
    sh_map = (lambda b, t: (0, 0, 0)) if shared else (lambda b, t: (b, 0, 0))
    s_map = (lambda b, t: (0, 0, 0, 0)) if shared else (lambda b, t: (b, 0, 0, 0))
    tile = pl.BlockSpec((1, tm, D), lambda b, t: (b, t, 0))
    return pl.pallas_call(
        functools.partial(_rwkv_seq_kernel, tm=tm, L=L),
        grid=(B, T // tm),
        in_specs=([tile, pl.BlockSpec((1, 1, D), sh_map), pl.BlockSpec((1, nh, HEAD, HEAD), s_map)]
                  + _rwkv_weight_specs(D, p) + [_const_spec((1, D))] * 3 + [_const_spec((D, D))]),
        out_specs=[tile, pl.BlockSpec((1, 1, D), lambda b, t: (b, 0, 0)),
                   pl.BlockSpec((1, nh, HEAD, HEAD), lambda b, t: (b, 0, 0, 0))],
        out_shape=[jax.ShapeDtypeStruct((B, T, D), F32), jax.ShapeDtypeStruct((B, 1, D), F32),
                   jax.ShapeDtypeStruct((B, nh, HEAD, HEAD), F32)],
        scratch_shapes=[pltpu.VMEM((1, D), F32), pltpu.VMEM((nh, HEAD, HEAD), F32)]
                       + [pltpu.VMEM((tm, D), F32)] * 8,
        compiler_params=_params(("arbitrary", "arbitrary")),
        name="rwkv_seq",
    )(h, sh0, s0, *_rwkv_weight_args(p), *post_p)


def _wkv_step_kernel(r_ref, lw_ref, k_ref, v_ref, a_ref, b_ref, s_ref, y_ref, sout_ref):
    S = s_ref[0]
    sa = jnp.sum(S * a_ref[...], axis=1, keepdims=True)
    S = S * jnp.exp(lw_ref[...]) + sa * b_ref[...] + v_ref[0] * k_ref[...]
    sout_ref[0] = S
    y_ref[0] = jnp.sum(S * r_ref[...], axis=1, keepdims=True)


def _wkv_step(r, lw, k, v, a, b, s_t):
    R, D = r.shape
    nh = D // HEAD
    rows = lambda x: x.T.reshape(nh, HEAD, R)
    row_spec = pl.BlockSpec((1, HEAD, R), lambda h: (h, 0, 0))
    col_spec = pl.BlockSpec((1, HEAD, 1, R), lambda h: (h, 0, 0, 0))
    st_spec = pl.BlockSpec((1, HEAD, HEAD, R), lambda h: (h, 0, 0, 0))
    y, s_new = pl.pallas_call(
        _wkv_step_kernel,
        grid=(nh,),
        in_specs=[row_spec, row_spec, row_spec, col_spec, row_spec, row_spec, st_spec],
        out_specs=[col_spec, st_spec],
        out_shape=[jax.ShapeDtypeStruct((nh, HEAD, 1, R), F32), jax.ShapeDtypeStruct((nh, HEAD, HEAD, R), F32)],
        compiler_params=_params(("arbitrary",)),
        name="wkv_step",
    )(rows(r), rows(lw), rows(k), v.T.reshape(nh, HEAD, 1, R), rows(a), rows(b), s_t)
    return y.reshape(D, R).T, s_new


def _rwkv_post_kernel(y_ref, r_ref, k_ref, v_ref, g_ref, h_ref, gng_ref, gnb_ref, rk_ref, wo_ref, o_ref):
    o_ref[...] = _rwkv_post_math(y_ref[...], r_ref[...], k_ref[...], v_ref[...], g_ref[...], h_ref[...],
                                 gng_ref[...], gnb_ref[...], rk_ref[...], wo_ref[...])


def _rwkv_post(y, r, k, v, g, h, gng, gnb, rk, wo, *, tm):
    R, D = y.shape
    tile = pl.BlockSpec((tm, D), lambda i: (i, 0))
    return pl.pallas_call(
        _rwkv_post_kernel,
        grid=(R // tm,),
        in_specs=[tile] * 6 + [_const_spec((1, D))] * 3 + [_const_spec((D, D))],
        out_specs=tile,
        out_shape=jax.ShapeDtypeStruct((R, D), F32),
        compiler_params=_params(("arbitrary",)),
        name="rwkv_post",
    )(y, r, k, v, g, h, gng, gnb, rk, wo)


def _row_tile(rows, target):
    tm = min(rows, target)
    while rows % tm:
        tm //= 2
    return tm


def kernel(x_prompt, x_sample, state_conv, state_shift, state_wkv, meta_tokens, norm_mix, norm_mlp, norm_final, conv_w_pw1, conv_b_pw1, conv_w_dw, conv_b_dw, conv_ln_g, conv_ln_b, conv_w_pw2, conv_b_pw2, rwkv_x_mix, rwkv_w_r, rwkv_w_k, rwkv_w_v, rwkv_w_o, rwkv_w0, rwkv_w1, rwkv_w2, rwkv_a0, rwkv_a1, rwkv_a2, rwkv_g1, rwkv_g2, rwkv_k_k, rwkv_k_a, rwkv_r_k, rwkv_gn_g, rwkv_gn_b, w_mlp_in, w_mlp_out):
    B, T, D = x_prompt.shape
    SB = x_sample.shape[0]
    nh = D // HEAD
    depth = norm_mix.shape[0]
    assert depth == 2 and x_sample.shape[1] == 1 and D % MXU_TILE == 0

    row = lambda x: x.reshape(1, -1).astype(F32)
    bf = lambda x: x.astype(BF16)

    conv_p = (row(norm_mix[0]), bf(conv_w_pw1[0]), row(conv_b_pw1[0]), conv_w_dw[0], row(conv_b_dw[0]),
              row(conv_ln_g[0]), row(conv_ln_b[0]), bf(conv_w_pw2[0]), row(conv_b_pw2[0]))
    rwkv_p = dict(gn=row(norm_mix[1]), xmix=rwkv_x_mix[0], wr=bf(rwkv_w_r[0]), wk=bf(rwkv_w_k[0]),
                  wv=bf(rwkv_w_v[0]), w0=row(rwkv_w0[0]), w1=bf(rwkv_w1[0]), w2=bf(rwkv_w2[0]),
                  a0=row(rwkv_a0[0]), a1=bf(rwkv_a1[0]), a2=bf(rwkv_a2[0]), g1=bf(rwkv_g1[0]),
                  g2=bf(rwkv_g2[0]), kk=row(rwkv_k_k[0]), ka=row(rwkv_k_a[0]))
    post_p = (row(rwkv_gn_g[0]), row(rwkv_gn_b[0]), row(rwkv_r_k[0]), bf(rwkv_w_o[0]))
    mlp_p = [(row(norm_mlp[i]), bf(w_mlp_in[i]), bf(w_mlp_out[i])) for i in range(depth)]
    gf = row(norm_final)

    def mlp(x2d, i, final):
        return _mlp(x2d, *mlp_p[i], gf, final=final, tm=_row_tile(x2d.shape[0], 512))

    def seq_trunk(x, buf0, sh0, s0, need_out):
        b, t, _ = x.shape
        h, nb = _conv_seq(x, buf0, *conv_p, *mlp_p[0], tm=_row_tile(t, 512))
        h, sh, s1 = _rwkv_seq(h, sh0, s0, rwkv_p, post_p, tm=_row_tile(t, 512), L=_row_tile(t, 64))
        if not need_out:
            return None, nb, sh, s1
        return mlp(h.reshape(b * t, D), 1, True).reshape(b, t, D), nb, sh, s1

    zero_buf = jnp.zeros((1, BUF_PAD, D), F32)
    zero_sh = jnp.zeros((1, 1, D), F32)
    zero_s = jnp.zeros((1, nh, HEAD, HEAD), F32)
    _, m_buf, m_sh, m_s = seq_trunk(meta_tokens.astype(F32)[None], zero_buf, zero_sh, zero_s, False)

    y_prompt, p_buf, p_sh, p_s = seq_trunk(x_prompt, m_buf, m_sh, m_s, True)
    conv_prompt = p_buf[:, BUF_PAD - CONV_BUF:][None]
    shift_prompt = p_sh.reshape(1, B, D)
    wkv_prompt = p_s[None].astype(state_wkv.dtype)

    xs = x_sample.reshape(SB, D)
    h, hist_t = _conv_step(xs, jnp.swapaxes(state_conv[0], 0, 1), *conv_p, rb=6)
    conv_sample = jnp.swapaxes(hist_t, 0, 1)[None]
    h = mlp(h, 0, False)
    r, lw, k, v, a, bb, g, sh = _rwkv_pre_step(h, state_shift[0], rwkv_p)
    y, s_t = _wkv_step(r, lw, k, v, a, bb, jnp.transpose(state_wkv[0].astype(F32), (1, 2, 3, 0)))
    s_new = jnp.transpose(s_t, (3, 0, 1, 2))
    h = _rwkv_post(y, r, k, v, g, h, *post_p, tm=SB)
    y_sample = mlp(h, 1, True).reshape(SB, 1, D)
    shift_sample = sh[None]
    wkv_sample = s_new[None].astype(state_wkv.dtype)

    return (y_prompt, y_sample, conv_prompt, shift_prompt, wkv_prompt,
            conv_sample, shift_sample, wkv_sample)
```

```python
import functools
import math

import jax
import jax.numpy as jnp
from jax import lax
from jax.experimental import pallas as pl
from jax.experimental.pallas import tpu as pltpu

F32 = jnp.float32
BF16 = jnp.bfloat16

RMS_EPS = 1e-6
LN_EPS = 1e-5
GN_EPS = 64e-5
L2_EPS = 1e-12
HEAD = 64
CONV_W = 31
CONV_BUF = CONV_W - 1
BUF_PAD = 32
CONV_SUBTILE = 256
MXU_TILE = 256
LANE = 128
SUBLANE = 8
VMEM_LIMIT = 52 * 1024 * 1024


def _mm(a, w):
    return jnp.dot(a.astype(BF16), w, preferred_element_type=F32)


def _dot_nt(a, b):
    return lax.dot_general(a, b, (((1,), (1,)), ((), ())), preferred_element_type=F32)


def _dot_tn(a, b):
    return lax.dot_general(a, b, (((0,), (0,)), ((), ())), preferred_element_type=F32)


def _rmsnorm(x, g):
    ms = jnp.mean(x * x, axis=-1, keepdims=True)
    return x * lax.rsqrt(ms + RMS_EPS) * g


def _head_ones():
    r = lax.broadcasted_iota(jnp.int32, (MXU_TILE, MXU_TILE), 0) // HEAD
    c = lax.broadcasted_iota(jnp.int32, (MXU_TILE, MXU_TILE), 1) // HEAD
    return jnp.where(r == c, 1.0, 0.0).astype(BF16)


def _seg_sum(x, ones_bd):
    hi = x.astype(BF16)
    lo = (x - hi.astype(F32)).astype(BF16)
    outs = []
    for c in range(x.shape[-1] // MXU_TILE):
        sl = slice(c * MXU_TILE, (c + 1) * MXU_TILE)
        s = jnp.dot(hi[:, sl], ones_bd, preferred_element_type=F32)
        s = s + jnp.dot(lo[:, sl], ones_bd, preferred_element_type=F32)
        outs.append(s)
    return jnp.concatenate(outs, axis=-1)


def _const_spec(shape):
    nd = len(shape)
    return pl.BlockSpec(shape, lambda *_: (0,) * nd)


def _weight_spec(shape):
    nd = len(shape)
    return pl.BlockSpec(shape, lambda *_: (0,) * nd, pipeline_mode=pl.Buffered(1))


def _params(sem):
    return pltpu.CompilerParams(dimension_semantics=sem, vmem_limit_bytes=VMEM_LIMIT)


def _conv_pre(x, g, w1, b1):
    d = x.shape[-1]
    u = _mm(_rmsnorm(x, g), w1) + b1
    return u[:, :d] * jax.nn.sigmoid(u[:, d:])


def _conv_post(x, c, lng, lnb, w2, b2):
    mu = jnp.mean(c, axis=-1, keepdims=True)
    cc = c - mu
    var = jnp.mean(cc * cc, axis=-1, keepdims=True)
    cn = cc * lax.rsqrt(var + LN_EPS) * lng + lnb
    return x + _mm(cn * jax.nn.sigmoid(cn), w2) + b2


def _conv_seq_kernel(x_ref, buf0_ref, g_ref, w1_ref, b1_ref, wdw_ref, bdw_ref, lng_ref, lnb_ref,
                     w2_ref, b2_ref, gm_ref, win_ref, wout_ref, h_ref, nb_ref, ubuf, cbuf, shl, *, tm, ts):
    t = pl.program_id(1)
    d = x_ref.shape[-1]

    @pl.when(t == 0)
    def _():
        ubuf[0:BUF_PAD, :] = buf0_ref[0]

    x = x_ref[0]
    ubuf[BUF_PAD:BUF_PAD + tm, :] = _conv_pre(x, g_ref[...], w1_ref[...], b1_ref[...])

    off = BUF_PAD - CONV_BUF
    span = ts + BUF_PAD - SUBLANE

    def conv(i):
        r0 = i * ts
        for l in range(d // LANE):
            ls = slice(l * LANE, (l + 1) * LANE)
            for p in range(1, SUBLANE):
                shl[p - 1] = ubuf[r0 + p:r0 + p + span, ls]
            acc = jnp.broadcast_to(bdw_ref[:, ls], (ts, LANE))
            for j in range(CONV_W):
                q, p = divmod(off + j, SUBLANE)
                if p == 0:
                    src = ubuf[r0 + SUBLANE * q:r0 + SUBLANE * q + ts, ls]
                else:
                    src = shl[p - 1, SUBLANE * q:SUBLANE * q + ts, :]
                acc = acc + wdw_ref[j:j + 1, ls] * src
            cbuf[r0:r0 + ts, ls] = acc

    for i in range(tm // ts):
        conv(i)
    h = _conv_post(x, cbuf[...], lng_ref[...], lnb_ref[...], w2_ref[...], b2_ref[...])
    h_ref[0] = _mlp_math(h, gm_ref[...], win_ref[...], wout_ref[...])
    tail = ubuf[tm:tm + BUF_PAD, :]
    ubuf[0:BUF_PAD, :] = tail

    @pl.when(t == pl.num_programs(1) - 1)
    def _():
        nb_ref[0] = tail


def _conv_seq(x, buf0, g, w1, b1, wdw, bdw, lng, lnb, w2, b2, gm, win, wout, *, tm):
    B, T, D = x.shape
    F = win.shape[1]
    nt = T // tm
    ts = min(tm, CONV_SUBTILE)
    shared = buf0.shape[0] == 1
    buf_map = (lambda b, t: (0, 0, 0)) if shared else (lambda b, t: (b, 0, 0))
    return pl.pallas_call(
        functools.partial(_conv_seq_kernel, tm=tm, ts=ts),
        grid=(B, nt),
        in_specs=[
            pl.BlockSpec((1, tm, D), lambda b, t: (b, t, 0)),
            pl.BlockSpec((1, BUF_PAD, D), buf_map),
            _const_spec((1, D)), _weight_spec((D, 2 * D)), _const_spec((1, 2 * D)),
            _const_spec((CONV_W, D)), _const_spec((1, D)), _const_spec((1, D)), _const_spec((1, D)),
            _weight_spec((D, D)), _const_spec((1, D)),
            _const_spec((1, D)), _weight_spec((D, F)), _weight_spec((F, D)),
        ],
        out_specs=[
            pl.BlockSpec((1, tm, D), lambda b, t: (b, t, 0)),
            pl.BlockSpec((1, BUF_PAD, D), lambda b, t: (b, 0, 0)),
        ],
        out_shape=[jax.ShapeDtypeStruct((B, T, D), F32), jax.ShapeDtypeStruct((B, BUF_PAD, D), F32)],
        scratch_shapes=[pltpu.VMEM((BUF_PAD + tm, D), F32), pltpu.VMEM((tm, D), F32),
                        pltpu.VMEM((SUBLANE - 1, ts + BUF_PAD - SUBLANE, LANE), F32)],
        compiler_params=_params(("arbitrary", "arbitrary")),
        name="conv_seq",
    )(x, buf0, g, w1, b1, wdw, bdw, lng, lnb, w2, b2, gm, win, wout)


def _conv_step_kernel(x_ref, buf_ref, wj_ref, wlast_ref, g_ref, w1_ref, b1_ref, bdw_ref, lng_ref, lnb_ref,
                      w2_ref, b2_ref, h_ref, nb_ref, ubuf, cbuf, keep, *, rb):
    i = pl.program_id(0)
    n_blk = pl.num_programs(0) - 1

    @pl.when(i == 0)
    def _():
        u = _conv_pre(x_ref[...], g_ref[...], w1_ref[...], b1_ref[...])
        ubuf[...] = u
        cbuf[...] = bdw_ref[...] + wlast_ref[...] * u

    @pl.when(i < n_blk)
    def _():
        acc = cbuf[...]
        for r in range(rb):
            acc = acc + wj_ref[r] * buf_ref[r]
        cbuf[...] = acc

    @pl.when(jnp.logical_and(i >= 1, i < n_blk))
    def _():
        nb_ref[0:rb - 1] = keep[...]
        nb_ref[rb - 1] = buf_ref[0]

    @pl.when(i < n_blk)
    def _():
        keep[...] = buf_ref[1:rb]

    @pl.when(i == n_blk)
    def _():
        nb_ref[0:rb - 1] = keep[...]
        nb_ref[rb - 1] = ubuf[...]
        h_ref[...] = _conv_post(x_ref[...], cbuf[...], lng_ref[...], lnb_ref[...], w2_ref[...], b2_ref[...])


def _conv_step(x, buf_t, g, w1, b1, wdw, bdw, lng, lnb, w2, b2, *, rb):
    R, D = x.shape
    n_hist = buf_t.shape[0]
    assert n_hist % rb == 0 and rb >= 2
    last = n_hist // rb - 1
    return pl.pallas_call(
        functools.partial(_conv_step_kernel, rb=rb),
        grid=(n_hist // rb + 1,),
        in_specs=[
            _const_spec((R, D)),
            pl.BlockSpec((rb, R, D), lambda i: (jnp.minimum(i, last), 0, 0)),
            pl.BlockSpec((rb, 1, D), lambda i: (jnp.minimum(i, last), 0, 0)),
            _const_spec((1, D)),
            _const_spec((1, D)), _const_spec((D, 2 * D)), _const_spec((1, 2 * D)),
            _const_spec((1, D)), _const_spec((1, D)), _const_spec((1, D)),
            _const_spec((D, D)), _const_spec((1, D)),
        ],
        out_specs=[_const_spec((R, D)), pl.BlockSpec((rb, R, D), lambda i: (jnp.maximum(i - 1, 0), 0, 0))],
        out_shape=[jax.ShapeDtypeStruct((R, D), F32), jax.ShapeDtypeStruct((n_hist, R, D), F32)],
        scratch_shapes=[pltpu.VMEM((R, D), F32), pltpu.VMEM((R, D), F32), pltpu.VMEM((rb - 1, R, D), F32)],
        compiler_params=_params(("arbitrary",)),
        name="conv_step",
    )(x, buf_t, wdw[:n_hist, None, :], wdw[n_hist:], g, w1, b1, bdw, lng, lnb, w2, b2)


def _mlp_math(x, g, win, wout):
    hid = jnp.maximum(_mm(_rmsnorm(x, g), win), 0.0)
    return x + _mm(hid * hid, wout)


def _mlp_kernel(x_ref, g_ref, win_ref, wout_ref, gf_ref, o_ref, *, final):
    y = _mlp_math(x_ref[...], g_ref[...], win_ref[...], wout_ref[...])
    if final:
        y = _rmsnorm(y, gf_ref[...])
    o_ref[...] = y


def _mlp(x, g, win, wout, gf, *, final, tm):
    R, D = x.shape
    F = win.shape[1]
    return pl.pallas_call(
        functools.partial(_mlp_kernel, final=final),
        grid=(R // tm,),
        in_specs=[
            pl.BlockSpec((tm, D), lambda i: (i, 0)),
            _const_spec((1, D)), _weight_spec((D, F)), _weight_spec((F, D)), _const_spec((1, D)),
        ],
        out_specs=pl.BlockSpec((tm, D), lambda i: (i, 0)),
        out_shape=jax.ShapeDtypeStruct((R, D), F32),
        compiler_params=_params(("arbitrary",)),
        name="mlp",
    )(x, g, win, wout, gf)


def _mlp_stream_kernel(x_ref, g_ref, win_ref, wout_ref, gf_ref, o_ref, hn, acc, *, final):
    c = pl.program_id(0)

    @pl.when(c == 0)
    def _():
        x = x_ref[...]
        hn[...] = _rmsnorm(x, g_ref[...]).astype(BF16)
        acc[...] = x

    hid = jnp.maximum(jnp.dot(hn[...], win_ref[...], preferred_element_type=F32), 0.0)
    acc[...] += _mm(hid * hid, wout_ref[...])

    @pl.when(c == pl.num_programs(0) - 1)
    def _():
        y = acc[...]
        if final:
            y = _rmsnorm(y, gf_ref[...])
        o_ref[...] = y


def _mlp_stream(x, g, win, wout, gf, *, final, fc):
    R, D = x.shape
    F = win.shape[1]
    return pl.pallas_call(
        functools.partial(_mlp_stream_kernel, final=final),
        grid=(F // fc,),
        in_specs=[
            _const_spec((R, D)), _const_spec((1, D)),
            pl.BlockSpec((D, fc), lambda c: (0, c)), pl.BlockSpec((fc, D), lambda c: (c, 0)),
            _const_spec((1, D)),
        ],
        out_specs=_const_spec((R, D)),
        out_shape=jax.ShapeDtypeStruct((R, D), F32),
        scratch_shapes=[pltpu.VMEM((R, D), BF16), pltpu.VMEM((R, D), F32)],
        compiler_params=_params(("arbitrary",)),
        name="mlp_stream",
    )(x, g, win, wout, gf)


def _rwkv_pre_math(hn, prev, xmix, wr, wk, wv, w0, w1, w2, a0, a1, a2, g1, g2, kk_w, ka_w):
    xx = prev - hn
    xr = hn + xx * xmix[0:1]
    xw = hn + xx * xmix[1:2]
    xk = hn + xx * xmix[2:3]
    xv = hn + xx * xmix[3:4]
    xa = hn + xx * xmix[4:5]
    xg = hn + xx * xmix[5:6]
    r = _mm(xr, wr)
    k = _mm(xk, wk)
    v = _mm(xv, wv)
    z = w0 + _mm(jnp.tanh(_mm(xw, w1)), w2)
    w_log = -(jnp.maximum(-z, 0.0) + jnp.log(1.0 + jnp.exp(-jnp.abs(z)))) - 0.5
    lw = -jnp.exp(w_log)
    iclr = jax.nn.sigmoid(a0 + _mm(_mm(xa, a1), a2))
    gate = _mm(jax.nn.sigmoid(_mm(xg, g1)), g2)
    kk = k * kk_w
    nrm = jnp.sqrt(_seg_sum(kk * kk, _head_ones()))
    kk = kk / jnp.maximum(nrm, L2_EPS)
    k2 = k * (1.0 + (iclr - 1.0) * ka_w)
    return r, lw, k2, v, -kk, kk * iclr, gate


def _rwkv_weight_specs(D, p):
    return [
        _const_spec((1, D)), _const_spec((6, D)),
        _const_spec((D, D)), _const_spec((D, D)), _const_spec((D, D)),
        _const_spec((1, D)), _const_spec(p["w1"].shape), _const_spec(p["w2"].shape),
        _const_spec((1, D)), _const_spec(p["a1"].shape), _const_spec(p["a2"].shape),
        _const_spec(p["g1"].shape), _const_spec(p["g2"].shape),
        _const_spec((1, D)), _const_spec((1, D)),
    ]


def _rwkv_weight_args(p):
    return (p["gn"], p["xmix"], p["wr"], p["wk"], p["wv"], p["w0"], p["w1"], p["w2"],
            p["a0"], p["a1"], p["a2"], p["g1"], p["g2"], p["kk"], p["ka"])


def _rwkv_pre_step_kernel(h_ref, prev_ref, gn_ref, xmix_ref, wr_ref, wk_ref, wv_ref, w0_ref, w1_ref, w2_ref,
                          a0_ref, a1_ref, a2_ref, g1_ref, g2_ref, kk_ref, ka_ref,
                          r_ref, lw_ref, k_ref, v_ref, a_ref, b_ref, g_ref, sh_ref):
    hn = _rmsnorm(h_ref[...], gn_ref[...])
    outs = _rwkv_pre_math(hn, prev_ref[...], xmix_ref[...], wr_ref[...], wk_ref[...], wv_ref[...], w0_ref[...],
                          w1_ref[...], w2_ref[...], a0_ref[...], a1_ref[...], a2_ref[...], g1_ref[...],
                          g2_ref[...], kk_ref[...], ka_ref[...])
    for o_ref, o in zip((r_ref, lw_ref, k_ref, v_ref, a_ref, b_ref, g_ref), outs):
        o_ref[...] = o
    sh_ref[...] = hn


def _rwkv_pre_step(h, prev, p):
    R, D = h.shape
    full = _const_spec((R, D))
    return pl.pallas_call(
        _rwkv_pre_step_kernel,
        grid=(1,),
        in_specs=[full, full] + _rwkv_weight_specs(D, p),
        out_specs=[full] * 8,
        out_shape=[jax.ShapeDtypeStruct((R, D), F32)] * 8,
        compiler_params=_params(("arbitrary",)),
        name="rwkv_pre_step",
    )(h, prev, *_rwkv_weight_args(p))


def _rwkv_post_math(y, r, k, v, g, h, gng, gnb, rk, wo):
    ones_bd = _head_ones()
    inv_n = 1.0 / HEAD
    mu = _seg_sum(y, ones_bd) * inv_n
    yc = y - mu
    var = _seg_sum(yc * yc, ones_bd) * inv_n
    yn = yc * lax.rsqrt(var + GN_EPS) * gng + gnb
    bonus = _seg_sum(r * k * rk, ones_bd) * v
    return h + _mm((yn + bonus) * g, wo)


def _wkv_chunk(r, lw, k, v, a, b, state, *, L):
    nh = state.shape[0]
    row = lax.broadcasted_iota(jnp.int32, lw.shape, 0)
    cs = lw
    s = 1
    while s < L:
        cs = cs + jnp.where(row >= s, pltpu.roll(cs, s, 0), 0.0)
        s *= 2
    gam = jnp.exp(cs)
    ginv = jnp.exp(-cs)
    g_last = gam[L - 1:L, :]
    rt_f = r * gam
    bt_f = b * ginv
    kt_f = k * ginv
    rt = rt_f.astype(BF16)
    at = (a * jnp.exp(cs - lw)).astype(BF16)
    bt = bt_f.astype(BF16)
    kt = kt_f.astype(BF16)
    bg = (bt_f * g_last).astype(BF16)
    kg = (kt_f * g_last).astype(BF16)
    vb = v.astype(BF16)

    ri = lax.broadcasted_iota(jnp.int32, (L, 2 * L), 0)
    ci = lax.broadcasted_iota(jnp.int32, (L, 2 * L), 1)
    left = ci < L
    cm = jnp.where(left, ci, ci - L)
    strict2 = ri > cm
    incl2 = ri >= cm
    eye_right = jnp.where(jnp.logical_and(ri == cm, jnp.logical_not(left)), 1.0, 0.0)
    zeros_l = jnp.zeros((L, HEAD), BF16)
    heads = range(nh)
    hs = lambda x, h: x[:, h * HEAD:(h + 1) * HEAD]
    dot = lambda x, y: jnp.dot(x, y, preferred_element_type=F32)
    cat0 = lambda x, y: jnp.concatenate([x, y], axis=0)
    cat1 = lambda x, y: jnp.concatenate([x, y], axis=1)

    bk_t = jnp.transpose(cat0(bt_f, kt_f)).astype(BF16)
    P = [dot(cat0(hs(at, h), hs(rt, h)), bk_t[h * HEAD:(h + 1) * HEAD, :]) for h in heads]
    m_a = [jnp.where(strict2, p[:L], 0.0) for p in P]
    m_r = [jnp.where(incl2, p[L:], 0.0).astype(BF16) for p in P]
    makv = [dot(m_a[h][:, L:].astype(BF16), hs(vb, h)).astype(BF16) for h in heads]
    W = [jnp.where(left, m, 0.0) + eye_right for m in m_a]
    for _ in range(int(math.log2(L))):
        Z = [dot(W[h][:, :L].astype(BF16), W[h].astype(BF16)) for h in heads]
        W = [jnp.where(left, Z[h], W[h] + Z[h]) for h in heads]
    tfin = [w[:, L:].astype(BF16) for w in W]
    TX = [dot(tfin[h], cat1(hs(at, h), makv[h])).astype(BF16) for h in heads]
    XV = [cat0(TX[h], cat1(zeros_l, hs(vb, h))) for h in heads]
    RY = [dot(m_r[h], XV[h]) for h in heads]
    rbar = [(hs(rt_f, h) + RY[h][:, :HEAD]).astype(BF16) for h in heads]
    GH = [_dot_tn(XV[h], cat0(hs(bg, h), hs(kg, h))) for h in heads]
    ys = []
    for h in heads:
        S = state[h]
        Sb = S.astype(BF16)
        ys.append(_dot_nt(rbar[h], Sb) + RY[h][:, HEAD:])
        state[h] = S * hs(g_last, h) + dot(Sb, GH[h][:HEAD].astype(BF16)) + GH[h][HEAD:]
    return ys


def _rwkv_seq_kernel(h_ref, sh0_ref, s0_ref, gn_ref, xmix_ref, wr_ref, wk_ref, wv_ref, w0_ref, w1_ref, w2_ref,
                     a0_ref, a1_ref, a2_ref, g1_ref, g2_ref, kk_ref, ka_ref, gng_ref, gnb_ref, rk_ref, wo_ref,
                     o_ref, sh_ref, sout_ref, carry, state, rb, lwb, kb, vb, ab, bb, gb, yb, *, tm, L):
    t = pl.program_id(1)

    @pl.when(t == 0)
    def _():
        carry[...] = sh0_ref[0]
        state[...] = s0_ref[0]

    h = h_ref[0]
    hn = _rmsnorm(h, gn_ref[...])
    row = lax.broadcasted_iota(jnp.int32, hn.shape, 0)
    prev = jnp.where(row == 0, carry[...], pltpu.roll(hn, 1, 0))
    last = hn[tm - 1:tm, :]
    carry[...] = last
    outs = _rwkv_pre_math(hn, prev, xmix_ref[...], wr_ref[...], wk_ref[...], wv_ref[...], w0_ref[...],
                          w1_ref[...], w2_ref[...], a0_ref[...], a1_ref[...], a2_ref[...], g1_ref[...],
                          g2_ref[...], kk_ref[...], ka_ref[...])
    for buf, o in zip((rb, lwb, kb, vb, ab, bb, gb), outs):
        buf[...] = o

    def chunk(c):
        rows = pl.ds(pl.multiple_of(c * L, L), L)
        ys = _wkv_chunk(rb[rows, :], lwb[rows, :], kb[rows, :], vb[rows, :], ab[rows, :], bb[rows, :], state, L=L)
        for i, y in enumerate(ys):
            yb[rows, i * HEAD:(i + 1) * HEAD] = y

    if tm == L:
        chunk(0)
    else:
        def body(c, carry_):
            chunk(c)
            return carry_
        lax.fori_loop(0, tm // L, body, 0, unroll=True)

    o_ref[0] = _rwkv_post_math(yb[...], rb[...], kb[...], vb[...], gb[...], h, gng_ref[...], gnb_ref[...],
                               rk_ref[...], wo_ref[...])

    @pl.when(t == pl.num_programs(1) - 1)
    def _():
        sh_ref[0] = last
        sout_ref[0] = state[...]


def _rwkv_seq(h, sh0, s0, p, post_p, *, tm, L):
    B, T, D = h.shape
    nh = D // HEAD
    shared = sh0.shape[0] == 1
    sh_map = (lambda b, t: (0, 0, 0)) if shared else (lambda b, t: (b, 0, 0))
    s_map = (lambda b, t: (0, 0, 0, 0)) if shared else (lambda b, t: (b, 0, 0, 0))
    tile = pl.BlockSpec((1, tm, D), lambda b, t: (b, t, 0))
    return pl.pallas_call(
        functools.partial(_rwkv_seq_kernel, tm=tm, L=L),
        grid=(B, T // tm),
        in_specs=([tile, pl.BlockSpec((1, 1, D), sh_map), pl.BlockSpec((1, nh, HEAD, HEAD), s_map)]
                  + _rwkv_weight_specs(D, p) + [_const_spec((1, D))] * 3 + [_const_spec((D, D))]),
        out_specs=[tile, pl.BlockSpec((1, 1, D), lambda b, t: (b, 0, 0)),
                   pl.BlockSpec((1, nh, HEAD, HEAD), lambda b, t: (b, 0, 0, 0))],
        out_shape=[jax.ShapeDtypeStruct((B, T, D), F32), jax.ShapeDtypeStruct((B, 1, D), F32),
                   jax.ShapeDtypeStruct((B, nh, HEAD, HEAD), F32)],
        scratch_shapes=[pltpu.VMEM((1, D), F32), pltpu.VMEM((nh, HEAD, HEAD), F32)]
                       + [pltpu.VMEM((tm, D), F32)] * 8,
        compiler_params=_params(("arbitrary", "arbitrary")),
        name="rwkv_seq",
    )(h, sh0, s0, *_rwkv_weight_args(p), *post_p)


def _wkv_step_kernel(r_ref, lw_ref, k_ref, v_ref, a_ref, b_ref, s_ref, y_ref, sout_ref):
    S = s_ref[0]
    sa = jnp.sum(S * a_ref[...], axis=1, keepdims=True)
    S = S * jnp.exp(lw_ref[...]) + sa * b_ref[...] + v_ref[0] * k_ref[...]
    sout_ref[0] = S
    y_ref[0] = jnp.sum(S * r_ref[...], axis=1, keepdims=True)


def _wkv_step(r, lw, k, v, a, b, s_t):
    R, D = r.shape
    nh = D // HEAD
    rows = lambda x: x.T.reshape(nh, HEAD, R)
    row_spec = pl.BlockSpec((1, HEAD, R), lambda h: (h, 0, 0))
    col_spec = pl.BlockSpec((1, HEAD, 1, R), lambda h: (h, 0, 0, 0))
    st_spec = pl.BlockSpec((1, HEAD, HEAD, R), lambda h: (h, 0, 0, 0))
    y, s_new = pl.pallas_call(
        _wkv_step_kernel,
        grid=(nh,),
        in_specs=[row_spec, row_spec, row_spec, col_spec, row_spec, row_spec, st_spec],
        out_specs=[col_spec, st_spec],
        out_shape=[jax.ShapeDtypeStruct((nh, HEAD, 1, R), F32), jax.ShapeDtypeStruct((nh, HEAD, HEAD, R), F32)],
        compiler_params=_params(("arbitrary",)),
        name="wkv_step",
    )(rows(r), rows(lw), rows(k), v.T.reshape(nh, HEAD, 1, R), rows(a), rows(b), s_t)
    return y.reshape(D, R).T, s_new


def _rwkv_post_kernel(y_ref, r_ref, k_ref, v_ref, g_ref, h_ref, gng_ref, gnb_ref, rk_ref, wo_ref, o_ref):
    o_ref[...] = _rwkv_post_math(y_ref[...], r_ref[...], k_ref[...], v_ref[...], g_ref[...], h_ref[...],
                                 gng_ref[...], gnb_ref[...], rk_ref[...], wo_ref[...])


def _rwkv_post(y, r, k, v, g, h, gng, gnb, rk, wo, *, tm):
    R, D = y.shape
    tile = pl.BlockSpec((tm, D), lambda i: (i, 0))
    return pl.pallas_call(
        _rwkv_post_kernel,
        grid=(R // tm,),
        in_specs=[tile] * 6 + [_const_spec((1, D))] * 3 + [_const_spec((D, D))],
        out_specs=tile,
        out_shape=jax.ShapeDtypeStruct((R, D), F32),
        compiler_params=_params(("arbitrary",)),
        name="rwkv_post",
    )(y, r, k, v, g, h, gng, gnb, rk, wo)


def _row_tile(rows, target):
    tm = min(rows, target)
    while rows % tm:
        tm //= 2
    return tm


def kernel(x_prompt, x_sample, state_conv, state_shift, state_wkv, meta_tokens, norm_mix, norm_mlp, norm_final, conv_w_pw1, conv_b_pw1, conv_w_dw, conv_b_dw, conv_ln_g, conv_ln_b, conv_w_pw2, conv_b_pw2, rwkv_x_mix, rwkv_w_r, rwkv_w_k, rwkv_w_v, rwkv_w_o, rwkv_w0, rwkv_w1, rwkv_w2, rwkv_a0, rwkv_a1, rwkv_a2, rwkv_g1, rwkv_g2, rwkv_k_k, rwkv_k_a, rwkv_r_k, rwkv_gn_g, rwkv_gn_b, w_mlp_in, w_mlp_out):
    B, T, D = x_prompt.shape
    SB = x_sample.shape[0]
    nh = D // HEAD
    depth = norm_mix.shape[0]
    assert depth == 2 and x_sample.shape[1] == 1 and D % MXU_TILE == 0

    row = lambda x: x.reshape(1, -1).astype(F32)
    bf = lambda x: x.astype(BF16)

    conv_p = (row(norm_mix[0]), bf(conv_w_pw1[0]), row(conv_b_pw1[0]), conv_w_dw[0], row(conv_b_dw[0]),
              row(conv_ln_g[0]), row(conv_ln_b[0]), bf(conv_w_pw2[0]), row(conv_b_pw2[0]))
    rwkv_p = dict(gn=row(norm_mix[1]), xmix=rwkv_x_mix[0], wr=bf(rwkv_w_r[0]), wk=bf(rwkv_w_k[0]),
                  wv=bf(rwkv_w_v[0]), w0=row(rwkv_w0[0]), w1=bf(rwkv_w1[0]), w2=bf(rwkv_w2[0]),
                  a0=row(rwkv_a0[0]), a1=bf(rwkv_a1[0]), a2=bf(rwkv_a2[0]), g1=bf(rwkv_g1[0]),
                  g2=bf(rwkv_g2[0]), kk=row(rwkv_k_k[0]), ka=row(rwkv_k_a[0]))
    post_p = (row(rwkv_gn_g[0]), row(rwkv_gn_b[0]), row(rwkv_r_k[0]), bf(rwkv_w_o[0]))
    mlp_p = [(row(norm_mlp[i]), bf(w_mlp_in[i]), bf(w_mlp_out[i])) for i in range(depth)]
    gf = row(norm_final)

    def mlp(x2d, i, final):
        if x2d.shape[0] <= MXU_TILE:
            return _mlp_stream(x2d, *mlp_p[i], gf, final=final, fc=2 * MXU_TILE)
        return _mlp(x2d, *mlp_p[i], gf, final=final, tm=_row_tile(x2d.shape[0], 512))

    def seq_trunk(x, buf0, sh0, s0, need_out):
        b, t, _ = x.shape
        h, nb = _conv_seq(x, buf0, *conv_p, *mlp_p[0], tm=_row_tile(t, 512))
        h, sh, s1 = _rwkv_seq(h, sh0, s0, rwkv_p, post_p, tm=_row_tile(t, 512), L=_row_tile(t, 64))
        if not need_out:
            return None, nb, sh, s1
        return mlp(h.reshape(b * t, D), 1, True).reshape(b, t, D), nb, sh, s1

    zero_buf = jnp.zeros((1, BUF_PAD, D), F32)
    zero_sh = jnp.zeros((1, 1, D), F32)
    zero_s = jnp.zeros((1, nh, HEAD, HEAD), F32)
    _, m_buf, m_sh, m_s = seq_trunk(meta_tokens.astype(F32)[None], zero_buf, zero_sh, zero_s, False)

    y_prompt, p_buf, p_sh, p_s = seq_trunk(x_prompt, m_buf, m_sh, m_s, True)
    conv_prompt = p_buf[:, BUF_PAD - CONV_BUF:][None]
    shift_prompt = p_sh.reshape(1, B, D)
    wkv_prompt = p_s[None].astype(state_wkv.dtype)

    xs = x_sample.reshape(SB, D)
    h, hist_t = _conv_step(xs, jnp.swapaxes(state_conv[0], 0, 1), *conv_p, rb=6)
    conv_sample = jnp.swapaxes(hist_t, 0, 1)[None]
    h = mlp(h, 0, False)
    r, lw, k, v, a, bb, g, sh = _rwkv_pre_step(h, state_shift[0], rwkv_p)
    y, s_t = _wkv_step(r, lw, k, v, a, bb, jnp.transpose(state_wkv[0].astype(F32), (1, 2, 3, 0)))
    s_new = jnp.transpose(s_t, (3, 0, 1, 2))
    h = _rwkv_post(y, r, k, v, g, h, *post_p, tm=SB)
    y_sample = mlp(h, 1, True).reshape(SB, 1, D)
    shift_sample = sh[None]
    wkv_sample = s_new[None].astype(state_wkv.dtype)

    return (y_prompt, y_sample, conv_prompt, shift_prompt, wkv_prompt,
            conv_sample, shift_sample, wkv_sample)
```
